```python
import math
import jax, jax.numpy as jnp
from jax import lax
import numpy as np

D_MODEL = 1024
BATCH = 4
SEQ = 4096
DEPTH = 4

N_EVEN = (DEPTH + 1) // 2
N_ODD = DEPTH // 2
HEAD_DIM = 64
MIX_HALF = D_MODEL // 2
ROPE_DIM = HEAD_DIM // 4
ROPE_THETA = 500000.0
EPS = 1e-6
NEG_INF = -1e30

NSA_HEADS = MIX_HALF // HEAD_DIM
NSA_KV_HEADS = max(1, NSA_HEADS // 4)
NSA_GROUP = NSA_HEADS // NSA_KV_HEADS
CMP_BLOCK = 32
CMP_STRIDE = 16
CMP_HIDDEN = 256
SLC_BLOCK = 64
SLC_TOPK = 16
WINDOW = 512
Q_BLOCK = 128
FORCE_SCORE = 1e3

GMLP_GROUPS = MIX_HALF // HEAD_DIM
GMLP_CHUNK = 128

RWKV_HEADS = MIX_HALF // HEAD_DIM
DECAY_LORA = 64
ICLR_LORA = 64
GATE_LORA = 128

LRU_WIDTH = MIX_HALF
LRU_BLOCKS = 8
LRU_CONV = 4
LRU_C = 8.0

D_FF = 11 * D_MODEL // 4
FFN_CONV = 3

NSA_Q = NSA_HEADS * HEAD_DIM
NSA_KV = NSA_KV_HEADS * HEAD_DIM
EVEN_SIZES = (NSA_Q,) + (NSA_KV,) * 6 + (3 * NSA_HEADS, MIX_HALF, MIX_HALF)
EVEN_COLS = sum(EVEN_SIZES)
RWKV_SIZES = (MIX_HALF, MIX_HALF, MIX_HALF, DECAY_LORA, ICLR_LORA, GATE_LORA)
RWKV_COLS = sum(RWKV_SIZES)
ODD_COLS = RWKV_COLS + 2 * LRU_WIDTH

kernel_name = 'hybrid_nsa_gmlp_rwkv7_rglru_block'


def rms_norm(x, g):
    xf = x.astype(jnp.float32)
    y = xf * lax.rsqrt(jnp.mean(xf * xf, axis=-1, keepdims=True) + EPS)
    return (y * g.astype(jnp.float32)).astype(x.dtype)


def layer_norm(x, g, b):
    xf = x.astype(jnp.float32)
    mu = jnp.mean(xf, axis=-1, keepdims=True)
    var = jnp.mean(jnp.square(xf - mu), axis=-1, keepdims=True)
    y = (xf - mu) * lax.rsqrt(var + EPS) * g.astype(jnp.float32) + b.astype(jnp.float32)
    return y.astype(x.dtype)


def modulate(h, shift, scale):
    return h * (1 + scale[:, None, :]) + shift[:, None, :]


def split_cols(p, sizes):
    offs = np.cumsum(sizes)[:-1].tolist()
    return jnp.split(p, offs, axis=-1)


def token_shift(z):
    return jnp.pad(z, ((0, 0), (1, 0), (0, 0)))[:, :-1]


def causal_dwconv(x, w, b):
    K, C = w.shape
    y = lax.conv_general_dilated(x, w[:, None, :].astype(x.dtype), window_strides=(1,),
                                 padding=[(K - 1, 0)], dimension_numbers=('NWC', 'WIO', 'NWC'),
                                 feature_group_count=C)
    return y + b.astype(x.dtype)


def rope_tables(positions):
    inv = ROPE_THETA ** (-jnp.arange(0, ROPE_DIM, 2, dtype=jnp.float32) / ROPE_DIM)
    ang = positions.astype(jnp.float32)[..., None] * inv
    return jnp.cos(ang), jnp.sin(ang)


def apply_rope(x, cos, sin):
    half = ROPE_DIM // 2
    xf = x.astype(jnp.float32)
    x1, x2, rest = xf[..., :half], xf[..., half:ROPE_DIM], xf[..., ROPE_DIM:]
    c, s = cos[:, :, None, :], sin[:, :, None, :]
    return jnp.concatenate([x1 * c - x2 * s, x2 * c + x1 * s, rest], axis=-1).astype(x.dtype)


def masked_softmax(s, mask):
    return jax.nn.softmax(jnp.where(mask, s.astype(jnp.float32), NEG_INF), axis=-1)


def compress_blocks(k, tok, pe, w1, w2):
    B, _, Hk, hd = k.shape
    n_cmp, L = tok.shape
    blk = k[:, tok] + pe[:, None, :].astype(k.dtype)
    blk = jnp.swapaxes(blk, 2, 3).reshape(B, n_cmp, Hk, L * hd)
    return jax.nn.gelu(blk @ w1) @ w2


def nsa_attention(q, k_c, v_c, k_s, v_s, k_w, v_w, gate_logits, cos, sin,
                  pe_k, w1_k, w2_k, pe_v, w1_v, w2_v):
    B, T, _ = q.shape
    H, Hk, G, hd = NSA_HEADS, NSA_KV_HEADS, NSA_GROUP, HEAD_DIM
    scale = hd ** -0.5
    q = apply_rope(q.reshape(B, T, H, hd), cos, sin).reshape(B, T, Hk, G, hd)
    k_c, v_c, k_s, v_s, k_w, v_w = [z.reshape(B, T, Hk, hd) for z in (k_c, v_c, k_s, v_s, k_w, v_w)]
    k_s = apply_rope(k_s, cos, sin)
    k_w = apply_rope(k_w, cos, sin)
    t_idx = jnp.arange(T)

    n_cmp = (T - CMP_BLOCK) // CMP_STRIDE + 1
    cmp_start = jnp.arange(n_cmp) * CMP_STRIDE
    cmp_end = cmp_start + CMP_BLOCK - 1
    tok = cmp_start[:, None] + jnp.arange(CMP_BLOCK)[None, :]
    kcmp = compress_blocks(k_c, tok, pe_k, w1_k, w2_k)
    vcmp = compress_blocks(v_c, tok, pe_v, w1_v, w2_v)
    kcmp = apply_rope(kcmp, cos[:, cmp_end], sin[:, cmp_end])
    s_c = jnp.einsum('btkgd,bnkd->bkgtn', q, kcmp).astype(jnp.float32) * scale
    valid_c = cmp_end[None, :] <= t_idx[:, None]
    p_c = masked_softmax(s_c, valid_c) * valid_c
    o_c = jnp.einsum('bkgtn,bnkd->btkgd', p_c.astype(vcmp.dtype), vcmp)

    n_slc = T // SLC_BLOCK
    k_top = min(SLC_TOPK, n_slc)
    slc_start = jnp.arange(n_slc) * SLC_BLOCK
    overlap = jnp.clip(jnp.minimum(cmp_start[:, None] + CMP_BLOCK, slc_start[None, :] + SLC_BLOCK)
                       - jnp.maximum(cmp_start[:, None], slc_start[None, :]), 0).astype(jnp.float32) / CMP_BLOCK
    imp = jnp.einsum('bkgtn,nj->bktj', p_c, overlap)
    cur = t_idx // SLC_BLOCK
    j = jnp.arange(n_slc)
    forced = (j[None, :] == 0) | (j[None, :] == cur[:, None]) | (j[None, :] == cur[:, None] - 1)
    causal_s = slc_start[None, :] <= t_idx[:, None]
    score = jnp.where(forced, FORCE_SCORE, jnp.where(causal_s, imp, -1.0))
    _, sel = lax.top_k(score, k_top)

    nq = T // Q_BLOCK
    ks_blk = jnp.swapaxes(k_s, 1, 2).reshape(B, Hk, n_slc, SLC_BLOCK, hd)
    vs_blk = jnp.swapaxes(v_s, 1, 2).reshape(B, Hk, n_slc, SLC_BLOCK, hd)
    pad = ((0, 0), (0, 0), (WINDOW, 0), (0, 0))
    kw_pad = jnp.pad(jnp.swapaxes(k_w, 1, 2), pad)
    vw_pad = jnp.pad(jnp.swapaxes(v_w, 1, 2), pad)
    q_ch = q.reshape(B, nq, Q_BLOCK, Hk, G, hd).transpose(1, 0, 3, 4, 2, 5)
    sel_ch = sel.reshape(B, Hk, nq, Q_BLOCK, k_top).transpose(2, 0, 1, 3, 4)
    starts = jnp.arange(nq) * Q_BLOCK
    b_i = jnp.arange(B)[:, None, None, None]
    h_i = jnp.arange(Hk)[None, :, None, None]
    n_sel_keys = k_top * SLC_BLOCK

    def query_block(args):
        qb, selb, s0 = args
        tq = s0 + jnp.arange(Q_BLOCK)
        kg = ks_blk[b_i, h_i, selb].reshape(B, Hk, Q_BLOCK, n_sel_keys, hd)
        vg = vs_blk[b_i, h_i, selb].reshape(B, Hk, Q_BLOCK, n_sel_keys, hd)
        kpos = (selb[..., None] * SLC_BLOCK + jnp.arange(SLC_BLOCK)).reshape(B, Hk, Q_BLOCK, n_sel_keys)
        m_s = (kpos <= tq[:, None])[:, :, None]
        s_s = jnp.einsum('bkgqd,bkqnd->bkgqn', qb, kg).astype(jnp.float32) * scale
        p_s = masked_softmax(s_s, m_s)
        o_s = jnp.einsum('bkgqn,bkqnd->bkgqd', p_s.astype(vg.dtype), vg)
        kwb = lax.dynamic_slice_in_dim(kw_pad, s0, Q_BLOCK + WINDOW, axis=2)
        vwb = lax.dynamic_slice_in_dim(vw_pad, s0, Q_BLOCK + WINDOW, axis=2)
        kp = s0 - WINDOW + jnp.arange(Q_BLOCK + WINDOW)
        m_w = (kp[None, :] <= tq[:, None]) & (kp[None, :] > tq[:, None] - WINDOW) & (kp[None, :] >= 0)
        s_w = jnp.einsum('bkgqd,bknd->bkgqn', qb, kwb).astype(jnp.float32) * scale
        p_w = masked_softmax(s_w, m_w)
        o_w = jnp.einsum('bkgqn,bknd->bkgqd', p_w.astype(vwb.dtype), vwb)
        return o_s, o_w

    o_s, o_w = lax.map(query_block, (q_ch, sel_ch, starts))
    o_s = o_s.transpose(1, 0, 4, 2, 3, 5).reshape(B, T, Hk, G, hd)
    o_w = o_w.transpose(1, 0, 4, 2, 3, 5).reshape(B, T, Hk, G, hd)
    gate = jax.nn.sigmoid(gate_logits.astype(jnp.float32)).reshape(B, T, Hk, G, 3).astype(q.dtype)
    out = gate[..., 0:1] * o_c + gate[..., 1:2] * o_s + gate[..., 2:3] * o_w
    return out.reshape(B, T, H * hd)


def gmlp_spatial_gating(u, v, ln_g, ln_b, ws, bs):
    B, T, W = u.shape
    gw = W // GMLP_GROUPS
    u = jax.nn.gelu(u)
    v = layer_norm(jax.nn.gelu(v).reshape(B, T, GMLP_GROUPS, gw), ln_g, ln_b)
    v = v.reshape(B, T // GMLP_CHUNK, GMLP_CHUNK, GMLP_GROUPS, gw)
    causal = jnp.tril(jnp.ones((GMLP_CHUNK, GMLP_CHUNK), dtype=bool))
    w = jnp.where(causal, ws, 0)
    mixed = jnp.einsum('gts,bcsgd->bctgd', w.astype(v.dtype), v) + bs.T[:, :, None].astype(v.dtype)
    return u * mixed.reshape(B, T, W)


def even_mixer(h, cos, sin, w_in, w_out, pe_k, w1_k, w2_k, pe_v, w1_v, w2_v, ln_g, ln_b, ws, bs):
    q, k_c, v_c, k_s, v_s, k_w, v_w, gates, u, v = split_cols(h @ w_in, EVEN_SIZES)
    y_a = nsa_attention(q, k_c, v_c, k_s, v_s, k_w, v_w, gates, cos, sin, pe_k, w1_k, w2_k, pe_v, w1_v, w2_v)
    y_b = gmlp_spatial_gating(u, v, ln_g, ln_b, ws, bs)
    return jnp.concatenate([y_a, y_b], axis=-1) @ w_out


def rwkv7_time_mix(r, k, v, wl, al, gl, w0, wB, a0, aB, gB, xi, alpha, rho, ln_g, ln_b):
    B, T, _ = r.shape
    H, hd = RWKV_HEADS, HEAD_DIM
    f32 = jnp.float32
    log_w = -math.exp(-0.5) * jax.nn.sigmoid((w0 + jnp.tanh(wl) @ wB).astype(f32))
    a = jax.nn.sigmoid((a0 + al @ aB).astype(f32))
    g = jax.nn.sigmoid(gl) @ gB
    heads = lambda z: z.astype(f32).reshape(B, T, H, hd)
    r_, k_, v_, a_, w_ = heads(r), heads(k), heads(v), heads(a), jnp.exp(heads(log_w))
    kappa = k_ * xi.astype(f32).reshape(H, hd)
    kappa = kappa * lax.rsqrt(jnp.sum(kappa * kappa, axis=-1, keepdims=True) + EPS)
    kt = k_ * (1 + (a_ - 1) * alpha.astype(f32).reshape(H, hd))

    def step(S, inp):
        w_t, kh_t, a_t, kt_t, v_t, r_t = inp
        Sk = jnp.einsum('bhvk,bhk->bhv', S, kh_t)
        S = S * w_t[:, :, None, :] - Sk[..., None] * (a_t * kh_t)[:, :, None, :] + v_t[..., None] * kt_t[:, :, None, :]
        return S, jnp.einsum('bhvk,bhk->bhv', S, r_t)

    xs = tuple(jnp.moveaxis(z, 1, 0) for z in (w_, kappa, a_, kt, v_, r_))
    _, y = lax.scan(step, jnp.zeros((B, H, hd, hd), f32), xs)
    y = jnp.moveaxis(y, 0, 1)
    bonus = jnp.sum(r_ * rho.astype(f32).reshape(H, hd) * kt, axis=-1, keepdims=True) * v_
    out = layer_norm(y, ln_g, ln_b) + bonus
    return g * out.reshape(B, T, H * hd).astype(g.dtype)


def rglru_branch(gate_in, x_in, conv_w, conv_b, wa, ba, wx, bx, lam):
    B, T, W = x_in.shape
    f32 = jnp.float32
    xc = causal_dwconv(x_in, conv_w, conv_b)
    xb = xc.reshape(B, T, LRU_BLOCKS, W // LRU_BLOCKS)
    r = jax.nn.sigmoid((jnp.einsum('btnd,nde->btne', xb, wa).reshape(B, T, W) + ba).astype(f32))
    i = jax.nn.sigmoid((jnp.einsum('btnd,nde->btne', xb, wx).reshape(B, T, W) + bx).astype(f32))
    log_a = -LRU_C * r * jax.nn.softplus(-lam.astype(f32))
    a = jnp.exp(log_a)
    b = jnp.sqrt(-jnp.expm1(2 * log_a)) * (i * xc.astype(f32))

    def combine(left, right):
        a1, b1 = left
        a2, b2 = right
        return a1 * a2, a2 * b1 + b2

    _, hseq = lax.associative_scan(combine, (a, b), axis=1)
    return jax.nn.gelu(gate_in) * hseq.astype(gate_in.dtype)


def odd_mixer(h, w_in, w_out, mu, w0, wB, a0, aB, gB, xi, alpha, rho, ln_g, ln_b,
              conv_w, conv_b, wa, ba, wx, bx, lam):
    p = h @ w_in
    pc, pd = p[..., :RWKV_COLS], p[..., RWKV_COLS:]
    pc = pc + mu * (token_shift(pc) - pc)
    r, k, v, wl, al, gl = split_cols(pc, RWKV_SIZES)
    y_c = rwkv7_time_mix(r, k, v, wl, al, gl, w0, wB, a0, aB, gB, xi, alpha, rho, ln_g, ln_b)
    gate_d, x_d = jnp.split(pd, 2, axis=-1)
    y_d = rglru_branch(gate_d, x_d, conv_w, conv_b, wa, ba, wx, bx, lam)
    return jnp.concatenate([y_c, y_d], axis=-1) @ w_out


def conv_glu_ffn(h, w_up, conv_w, conv_b, w_down):
    up = causal_dwconv(h @ w_up, conv_w, conv_b)
    a, b = jnp.split(up, 2, axis=-1)
    return (jax.nn.silu(a) * b) @ w_down


def setup_inputs(seed: int = 0) -> dict:
    key = jax.random.key(seed)
    ks = iter(jax.random.split(key, 64))
    nrm = lambda shape, s: s * jax.random.normal(next(ks), shape, jnp.float32)
    D, NE, NO, F2 = D_MODEL, N_EVEN, N_ODD, 2 * D_FF
    gw = MIX_HALF // LRU_BLOCKS
    u = jax.random.uniform(next(ks), (NO, LRU_WIDTH), jnp.float32, 0.9, 0.999)
    a_base = u ** (1.0 / LRU_C)
    lam = jnp.log(a_base) - jnp.log1p(-a_base)
    return {
        'x': nrm((BATCH, SEQ, D), 1.0),
        'c': nrm((BATCH, D), 1.0),
        'positions': jnp.broadcast_to(jnp.arange(SEQ, dtype=jnp.int32)[None, :], (BATCH, SEQ)),
        'ada_w': nrm((DEPTH, D, 6 * D), 0.5 * D ** -0.5),
        'ada_b': nrm((DEPTH, 6 * D), 0.1),
        'norm1_g': 1.0 + nrm((DEPTH, D), 0.1),
        'norm2_g': 1.0 + nrm((DEPTH, D), 0.1),
        'ev_w_in': nrm((NE, D, EVEN_COLS), D ** -0.5),
        'ev_w_out': nrm((NE, 2 * MIX_HALF, D), (2 * MIX_HALF) ** -0.5),
        'nsa_pe_k': nrm((NE, CMP_BLOCK, HEAD_DIM), 0.1),
        'nsa_w1_k': nrm((NE, CMP_BLOCK * HEAD_DIM, CMP_HIDDEN), (CMP_BLOCK * HEAD_DIM) ** -0.5),
        'nsa_w2_k': nrm((NE, CMP_HIDDEN, HEAD_DIM), CMP_HIDDEN ** -0.5),
        'nsa_pe_v': nrm((NE, CMP_BLOCK, HEAD_DIM), 0.1),
        'nsa_w1_v': nrm((NE, CMP_BLOCK * HEAD_DIM, CMP_HIDDEN), (CMP_BLOCK * HEAD_DIM) ** -0.5),
        'nsa_w2_v': nrm((NE, CMP_HIDDEN, HEAD_DIM), CMP_HIDDEN ** -0.5),
        'gmlp_ln_g': 1.0 + nrm((NE, GMLP_GROUPS, MIX_HALF // GMLP_GROUPS), 0.1),
        'gmlp_ln_b': nrm((NE, GMLP_GROUPS, MIX_HALF // GMLP_GROUPS), 0.1),
        'gmlp_ws': nrm((NE, GMLP_GROUPS, GMLP_CHUNK, GMLP_CHUNK), GMLP_CHUNK ** -0.5),
        'gmlp_bs': 1.0 + nrm((NE, GMLP_GROUPS, GMLP_CHUNK), 0.1),
        'od_w_in': nrm((NO, D, ODD_COLS), D ** -0.5),
        'od_w_out': nrm((NO, 2 * MIX_HALF, D), (2 * MIX_HALF) ** -0.5),
        'rwkv_mu': jax.random.uniform(next(ks), (NO, RWKV_COLS), jnp.float32),
        'rwkv_w0': nrm((NO, MIX_HALF), 0.5),
        'rwkv_wB': nrm((NO, DECAY_LORA, MIX_HALF), DECAY_LORA ** -0.5),
        'rwkv_a0': nrm((NO, MIX_HALF), 0.5),
        'rwkv_aB': nrm((NO, ICLR_LORA, MIX_HALF), ICLR_LORA ** -0.5),
        'rwkv_gB': nrm((NO, GATE_LORA, MIX_HALF), GATE_LORA ** -0.5),
        'rwkv_xi': 0.85 + nrm((NO, MIX_HALF), 0.1),
        'rwkv_alpha': 1.0 + nrm((NO, MIX_HALF), 0.1),
        'rwkv_rho': nrm((NO, MIX_HALF), 0.1),
        'rwkv_ln_g': 1.0 + nrm((NO, RWKV_HEADS, HEAD_DIM), 0.1),
        'rwkv_ln_b': nrm((NO, RWKV_HEADS, HEAD_DIM), 0.1),
        'lru_conv_w': nrm((NO, LRU_CONV, LRU_WIDTH), 0.5),
        'lru_conv_b': nrm((NO, LRU_WIDTH), 0.1),
        'lru_wa': nrm((NO, LRU_BLOCKS, gw, gw), gw ** -0.5),
        'lru_ba': nrm((NO, LRU_WIDTH), 0.1),
        'lru_wx': nrm((NO, LRU_BLOCKS, gw, gw), gw ** -0.5),
        'lru_bx': nrm((NO, LRU_WIDTH), 0.1),
        'lru_lambda': lam,
        'ffn_up': nrm((DEPTH, D, F2), D ** -0.5),
        'ffn_conv_w': nrm((DEPTH, FFN_CONV, F2), FFN_CONV ** -0.5),
        'ffn_conv_b': nrm((DEPTH, F2), 0.1),
        'ffn_down': nrm((DEPTH, D_FF, D), D_FF ** -0.5),
        'final_g': 1.0 + nrm((D,), 0.1),
    }


def reference(x, c, positions, ada_w, ada_b, norm1_g, norm2_g, ev_w_in, ev_w_out,
              nsa_pe_k, nsa_w1_k, nsa_w2_k, nsa_pe_v, nsa_w1_v, nsa_w2_v,
              gmlp_ln_g, gmlp_ln_b, gmlp_ws, gmlp_bs, od_w_in, od_w_out,
              rwkv_mu, rwkv_w0, rwkv_wB, rwkv_a0, rwkv_aB, rwkv_gB, rwkv_xi, rwkv_alpha, rwkv_rho,
              rwkv_ln_g, rwkv_ln_b, lru_conv_w, lru_conv_b, lru_wa, lru_ba, lru_wx, lru_bx, lru_lambda,
              ffn_up, ffn_conv_w, ffn_conv_b, ffn_down, final_g):
    cos, sin = rope_tables(positions)
    cond = jax.nn.silu(c)
    for layer in range(DEPTH):
        i = layer // 2
        mod = cond @ ada_w[layer] + ada_b[layer]
        sh1, sc1, g1, sh2, sc2, g2 = jnp.split(mod, 6, axis=-1)
        h = modulate(rms_norm(x, norm1_g[layer]), sh1, sc1)
        if layer % 2 == 0:
            y = even_mixer(h, cos, sin, ev_w_in[i], ev_w_out[i], nsa_pe_k[i], nsa_w1_k[i], nsa_w2_k[i],
                           nsa_pe_v[i], nsa_w1_v[i], nsa_w2_v[i], gmlp_ln_g[i], gmlp_ln_b[i],
                           gmlp_ws[i], gmlp_bs[i])
        else:
            y = odd_mixer(h, od_w_in[i], od_w_out[i], rwkv_mu[i], rwkv_w0[i], rwkv_wB[i], rwkv_a0[i],
                          rwkv_aB[i], rwkv_gB[i], rwkv_xi[i], rwkv_alpha[i], rwkv_rho[i], rwkv_ln_g[i],
                          rwkv_ln_b[i], lru_conv_w[i], lru_conv_b[i], lru_wa[i], lru_ba[i], lru_wx[i],
                          lru_bx[i], lru_lambda[i])
        x = x + g1[:, None, :] * y
        h = modulate(rms_norm(x, norm2_g[layer]), sh2, sc2)
        x = x + g2[:, None, :] * conv_glu_ffn(h, ffn_up[layer], ffn_conv_w[layer], ffn_conv_b[layer], ffn_down[layer])
    return rms_norm(x, final_g)
```

```python
import functools
import math

import numpy as np
import jax
import jax.numpy as jnp
from jax import lax
from jax.experimental import pallas as pl
from jax.experimental.pallas import tpu as pltpu

F32 = jnp.float32
BF16 = jnp.bfloat16

HEAD_DIM = 64
ROPE_DIM = HEAD_DIM // 4
ROPE_THETA = 500000.0
EPS = 1e-6
NEG_INF = -1e30
LANES = 128

NSA_GROUP = 4
CMP_BLOCK = 32
CMP_STRIDE = 16
SLC_BLOCK = 64
SLC_SHIFT = 6
SLC_TOPK = 16
WINDOW = 512
FORCE_SCORE = 1e3
GMLP_CHUNK = 128
RWKV_CHUNK = 64
LRU_C = 8.0

VMEM_LIMIT = 56 * 1024 * 1024


def _cp(n_axes, vmem=VMEM_LIMIT):
    return pltpu.CompilerParams(dimension_semantics=("arbitrary",) * n_axes, vmem_limit_bytes=vmem)


def _const_spec(shape):
    nd = len(shape)
    return pl.BlockSpec(shape, lambda *_: (0,) * nd, pipeline_mode=pl.Buffered(1))


def _dot(a, b):
    return jnp.dot(a, b, preferred_element_type=F32)


def _dot_nt(a, b):
    return lax.dot_general(a, b, (((1,), (1,)), ((), ())), preferred_element_type=F32)


def _dot_tn(a, b):
    return lax.dot_general(a, b, (((0,), (0,)), ((), ())), preferred_element_type=F32)


def _split2(z):
    hi = z.astype(BF16)
    lo = (z - hi.astype(F32)).astype(BF16)
    return hi, lo


def _split3(z):
    hi = z.astype(BF16)
    r = z - hi.astype(F32)
    mid = r.astype(BF16)
    lo = (r - mid.astype(F32)).astype(BF16)
    return hi, mid, lo


def _dot_exact_rhs(z, m):
    hi, mid, lo = _split3(z)
    return _dot(hi, m) + _dot(mid, m) + _dot(lo, m)


def _lhs_exact_dot(m, z):
    hi, mid, lo = _split3(z)
    return _dot(m, hi) + _dot(m, mid) + _dot(m, lo)


def _norm_mod(x, g, shift, scale):
    y = x * lax.rsqrt(jnp.mean(x * x, axis=-1, keepdims=True) + EPS)
    return (y * g) * (1.0 + scale) + shift


def _gelu(x):
    return jax.nn.gelu(x, approximate=True)


def _sigmoid(x):
    return jax.nn.sigmoid(x)


def _iota(shape, axis):
    return lax.broadcasted_iota(jnp.int32, shape, axis)


def _adaln_kernel(c_ref, w_ref, b_ref, o_ref):
    cond = c_ref[...]
    cond = cond * _sigmoid(cond)
    o_ref[0] = _dot(cond.astype(BF16), w_ref[0].astype(BF16)) + b_ref[0]


def adaln_mod(c, ada_w, ada_b):
    depth, d, n = ada_w.shape
    b = c.shape[0]
    rows = 16
    cp = jnp.zeros((rows, d), F32).at[:b].set(c)
    tn = 1536
    out = pl.pallas_call(
        _adaln_kernel,
        grid=(depth, n // tn),
        in_specs=[pl.BlockSpec((rows, d), lambda l, j: (0, 0)),
                  pl.BlockSpec((1, d, tn), lambda l, j: (l, 0, j)),
                  pl.BlockSpec((1, 1, tn), lambda l, j: (l, 0, j))],
        out_specs=pl.BlockSpec((1, rows, tn), lambda l, j: (l, 0, j)),
        out_shape=jax.ShapeDtypeStruct((depth, rows, n), F32),
        compiler_params=_cp(2),
    )(cp, ada_w, ada_b.reshape(depth, 1, n))
    return out[:, :b]


def _rope_kernel(pos_ref, inv_ref, c_ref, sa_ref, sb_ref):
    ang = pos_ref[0].astype(F32) * inv_ref[...]
    cos, sin = jnp.cos(ang), jnp.sin(ang)
    d = _iota(ang.shape, 1) % HEAD_DIM
    c_ref[0] = jnp.where(d < ROPE_DIM, cos, 1.0)
    sa_ref[0] = jnp.where(d < ROPE_DIM // 2, -sin, 0.0)
    sb_ref[0] = jnp.where((d >= ROPE_DIM // 2) & (d < ROPE_DIM), sin, 0.0)


def rope_tables(pos, tm):
    b, n = pos.shape
    half = ROPE_DIM // 2
    inv = ROPE_THETA ** (-jnp.arange(0, ROPE_DIM, 2, dtype=F32) / ROPE_DIM)
    inv_lane = jnp.tile(inv, LANES // half).reshape(1, LANES)
    spec = pl.BlockSpec((1, tm, LANES), lambda i, j: (i, j, 0))
    return pl.pallas_call(
        _rope_kernel,
        grid=(b, n // tm),
        in_specs=[pl.BlockSpec((1, tm, 1), lambda i, j: (i, j, 0)),
                  pl.BlockSpec((1, LANES), lambda i, j: (0, 0))],
        out_specs=[spec, spec, spec],
        out_shape=[jax.ShapeDtypeStruct((b, n, LANES), F32)] * 3,
        compiler_params=_cp(2),
    )(pos[..., None], inv_lane)


def _rope(y, c, sa, sb):
    return y * c + pltpu.roll(y, LANES - ROPE_DIM // 2, 1) * sa + pltpu.roll(y, ROPE_DIM // 2, 1) * sb


def _even_proj_kernel(x_ref, g_ref, sh_ref, sc_ref, w_ref, c_ref, sa_ref, sb_ref,
                      q_ref, kc_ref, vc_ref, ks_ref, vs_ref, kw_ref, vw_ref, gate_ref, uv_ref,
                      *, nq, nuv):
    h = _norm_mod(x_ref[0], g_ref[...], sh_ref[0], sc_ref[0]).astype(BF16)
    c, sa, sb = c_ref[0], sa_ref[0], sb_ref[0]
    scale = HEAD_DIM ** -0.5
    yq = _dot(h, w_ref[:, 0:nq])
    for j in range(nq // LANES):
        sl = slice(j * LANES, (j + 1) * LANES)
        q_ref[0, :, sl] = (_rope(yq[:, sl], c, sa, sb) * scale).astype(BF16)
    o = nq
    ykv = _dot(h, w_ref[:, o:o + 6 * LANES])
    kc_ref[0] = ykv[:, 0:LANES]
    vc_ref[0] = ykv[:, LANES:2 * LANES]
    ks_ref[0] = _rope(ykv[:, 2 * LANES:3 * LANES], c, sa, sb).astype(BF16)
    vs_ref[0] = ykv[:, 3 * LANES:4 * LANES].astype(BF16)
    kw_ref[0] = _rope(ykv[:, 4 * LANES:5 * LANES], c, sa, sb).astype(BF16)
    vw_ref[0] = ykv[:, 5 * LANES:6 * LANES].astype(BF16)
    o += 6 * LANES
    gate_ref[0] = _sigmoid(_dot(h, w_ref[:, o:o + 2 * LANES]))
    o += 2 * LANES
    uv_ref[0] = _dot(h, w_ref[:, o:o + nuv])


def even_proj(x, g, shift, scale, w, tabs, *, tm):
    b, t, d = x.shape
    n = w.shape[1]
    nq = 2 * NSA_GROUP * LANES
    nuv = n - nq - 8 * LANES
    row = lambda wdt: pl.BlockSpec((1, tm, wdt), lambda i, j: (i, j, 0))
    vec = pl.BlockSpec((1, 1, d), lambda i, j: (i, 0, 0))
    outs = [(nq, BF16), (LANES, F32), (LANES, F32), (LANES, BF16), (LANES, BF16), (LANES, BF16),
            (LANES, BF16), (2 * LANES, F32), (nuv, F32)]
    return pl.pallas_call(
        functools.partial(_even_proj_kernel, nq=nq, nuv=nuv),
        grid=(b, t // tm),
        in_specs=[row(d), _const_spec((1, d)), vec, vec, _const_spec((d, n)),
                  row(LANES), row(LANES), row(LANES)],
        out_specs=[row(wdt) for wdt, _ in outs],
        out_shape=[jax.ShapeDtypeStruct((b, t, wdt), dt) for wdt, dt in outs],
        compiler_params=_cp(2),
    )(x, g, shift, scale, w, *tabs)


def _compress_kernel(kc_ref, vc_ref, pek_ref, w1k_ref, w2k_ref, pev_ref, w1v_ref, w2v_ref,
                     c_ref, sa_ref, sb_ref, ko_ref, vo_ref, *, ngrp, n_cmp):
    half = CMP_BLOCK // 2

    def phi(src_ref, pe_ref, w1_ref, w2_ref):
        p = jnp.zeros((ngrp, w1_ref.shape[2]), F32)
        q = jnp.zeros((ngrp, w1_ref.shape[2]), F32)
        for l in range(half):
            a = src_ref[0, pl.ds(l, ngrp, stride=CMP_STRIDE), :]
            p = p + _dot((a + pe_ref[l:l + 1, :]).astype(BF16), w1_ref[l])
            q = q + _dot((a + pe_ref[half + l:half + l + 1, :]).astype(BF16), w1_ref[half + l])
        hid = p + pltpu.roll(q, ngrp - 1, 0)
        return _dot(_gelu(hid).astype(BF16), w2_ref[...])

    row = _iota((ngrp, LANES), 0)
    kc = _rope(phi(kc_ref, pek_ref, w1k_ref, w2k_ref), c_ref[0], sa_ref[0], sb_ref[0])
    ko_ref[0] = jnp.where(row < n_cmp, kc, 0.0).astype(BF16)
    vc = phi(vc_ref, pev_ref, w1v_ref, w2v_ref)
    vo_ref[0] = jnp.where(row < n_cmp, vc, 0.0).astype(BF16)


def nsa_compress(kc, vc, pek, w1k, w2k, pev, w1v, w2v, end_tabs):
    b, t, _ = kc.shape
    ngrp = t // CMP_STRIDE
    n_cmp = (t - CMP_BLOCK) // CMP_STRIDE + 1
    tok = pl.BlockSpec((1, t, LANES), lambda i: (i, 0, 0))
    grp = pl.BlockSpec((1, ngrp, LANES), lambda i: (i, 0, 0))
    return pl.pallas_call(
        functools.partial(_compress_kernel, ngrp=ngrp, n_cmp=n_cmp),
        grid=(b,),
        in_specs=[tok, tok, _const_spec(pek.shape), _const_spec(w1k.shape), _const_spec(w2k.shape),
                  _const_spec(pev.shape), _const_spec(w1v.shape), _const_spec(w2v.shape), grp, grp, grp],
        out_specs=[grp, grp],
        out_shape=[jax.ShapeDtypeStruct((b, ngrp, LANES), BF16)] * 2,
        compiler_params=_cp(1),
    )(kc, vc, pek, w1k, w2k, pev, w1v, w2v, *end_tabs)


def _cmp_topk_kernel(q_ref, kc_ref, vc_ref, ovt_ref, oc_ref, sel_ref, *, tq, n_cmp, ncp, n_slc, k_top):
    t0 = pl.program_id(2) * tq
    kc, vc = kc_ref[0], vc_ref[0]
    end = CMP_BLOCK - 1
    n_col, t_row = _iota((tq, ncp), 1), t0 + _iota((tq, ncp), 0)
    valid = (n_col * CMP_STRIDE + end <= t_row) & (n_col < n_cmp)
    n_row, t_col = _iota((ncp, tq), 0), t0 + _iota((ncp, tq), 1)
    valid_t = (n_row * CMP_STRIDE + end <= t_col) & (n_row < n_cmp)
    psum_t = jnp.zeros((ncp, tq), F32)
    for g in range(NSA_GROUP):
        qg = q_ref[0, :, g * LANES:(g + 1) * LANES]
        s = jnp.where(valid, _dot_nt(qg, kc), NEG_INF)
        e = jnp.where(valid, jnp.exp(s - jnp.max(s, axis=1, keepdims=True)), 0.0)
        l = jnp.sum(e, axis=1, keepdims=True)
        p = e / jnp.where(l > 0.0, l, 1.0)
        oc_ref[0, :, g * LANES:(g + 1) * LANES] = _dot(p.astype(BF16), vc)
        st = jnp.where(valid_t, _dot_nt(kc, qg), NEG_INF)
        et = jnp.where(valid_t, jnp.exp(st - jnp.max(st, axis=0, keepdims=True)), 0.0)
        lt = jnp.sum(et, axis=0, keepdims=True)
        psum_t = psum_t + et / jnp.where(lt > 0.0, lt, 1.0)
    hi, lo = _split2(psum_t)
    imp = _dot(ovt_ref[...], hi) + _dot(ovt_ref[...], lo)
    j = _iota((n_slc, tq), 0)
    t = t0 + _iota((n_slc, tq), 1)
    cur = jnp.right_shift(t, SLC_SHIFT)
    forced = (j == 0) | (j == cur) | (j == cur - 1)
    score = jnp.where(forced, FORCE_SCORE, jnp.where(j * SLC_BLOCK <= t, imp, -1.0))
    rank = jnp.zeros((n_slc, tq), jnp.int32)
    for jj in range(n_slc):
        r = score[jj:jj + 1, :]
        beats = (r > score) | ((r == score) & (j > jj))
        rank = rank + beats.astype(jnp.int32)
    sel_ref[0, 0] = jnp.where(rank < k_top, 1.0, 0.0)


def nsa_cmp_topk(q, kcmp, vcmp, *, tq):
    b, t, nq = q.shape
    hk = nq // (NSA_GROUP * LANES)
    ncp = kcmp.shape[1]
    n_cmp = (t - CMP_BLOCK) // CMP_STRIDE + 1
    n_slc = t // SLC_BLOCK
    cs = np.arange(ncp) * CMP_STRIDE
    ss = np.arange(n_slc) * SLC_BLOCK
    ov = np.clip(np.minimum(cs[:, None] + CMP_BLOCK, ss[None, :] + SLC_BLOCK)
                 - np.maximum(cs[:, None], ss[None, :]), 0, None).astype(np.float32) / CMP_BLOCK
    ov[n_cmp:] = 0.0
    ovt = jnp.asarray(ov.T, BF16)
    gq = NSA_GROUP * LANES
    return pl.pallas_call(
        functools.partial(_cmp_topk_kernel, tq=tq, n_cmp=n_cmp, ncp=ncp, n_slc=n_slc,
                          k_top=min(SLC_TOPK, n_slc)),
        grid=(b, hk, t // tq),
        in_specs=[pl.BlockSpec((1, tq, gq), lambda i, k, j: (i, j, k)),
                  pl.BlockSpec((1, ncp, LANES), lambda i, k, j: (i, 0, 0)),
                  pl.BlockSpec((1, ncp, LANES), lambda i, k, j: (i, 0, 0)),
                  _const_spec((n_slc, ncp))],
        out_specs=[pl.BlockSpec((1, tq, gq), lambda i, k, j: (i, j, k)),
                   pl.BlockSpec((1, 1, n_slc, tq), lambda i, k, j: (i, k, 0, j))],
        out_shape=[jax.ShapeDtypeStruct((b, t, nq), F32),
                   jax.ShapeDtypeStruct((b, hk, n_slc, t), F32)],
        compiler_params=_cp(3),
    )(q, kcmp, vcmp, ovt)


def _flash_step(q4, kb, vb, allow, carry, *, tq):
    m, l, acc = carry
    tk = kb.shape[0]
    s = _dot_nt(q4, kb).reshape(NSA_GROUP, tq, tk) + jnp.where(allow, 0.0, NEG_INF)[None]
    m_new = jnp.maximum(m, jnp.max(s, axis=2, keepdims=True))
    alpha = jnp.exp(m - m_new)
    p = jnp.exp(s - m_new) * jnp.where(allow, 1.0, 0.0)[None]
    l = alpha * l + jnp.sum(p, axis=2, keepdims=True)
    pv = _dot(p.reshape(NSA_GROUP * tq, tk).astype(BF16), vb)
    acc = alpha * acc + pv.reshape(NSA_GROUP, tq, LANES)
    return m_new, l, acc


def _sel_win_kernel(q_ref, ks_ref, vs_ref, kw_ref, vw_ref, sel_ref, oc_ref, gate_ref, y_ref,
                    *, tq, tk, n_slc):
    kv = pl.program_id(1)
    qi = pl.program_id(2)
    q0 = qi * tq
    q4 = jnp.concatenate([q_ref[0, :, g * LANES:(g + 1) * LANES] for g in range(NSA_GROUP)], axis=0)
    selm = sel_ref[0, 0]
    init = (jnp.full((NSA_GROUP, tq, 1), NEG_INF, F32), jnp.zeros((NSA_GROUP, tq, 1), F32),
            jnp.zeros((NSA_GROUP, tq, LANES), F32))

    def sel_body(kt, carry):
        k0 = pl.multiple_of(kt * tk, tk)
        expand = (jnp.right_shift(k0 + _iota((n_slc, tk), 1), SLC_SHIFT) == _iota((n_slc, tk), 0))
        picked = _dot(selm, jnp.where(expand, 1.0, 0.0).astype(BF16))
        allow = (picked > 0.5) & (k0 + _iota((tq, tk), 1) <= q0 + _iota((tq, tk), 0))
        return _flash_step(q4, ks_ref[0, pl.ds(k0, tk), :], vs_ref[0, pl.ds(k0, tk), :], allow, carry, tq=tq)

    _, l_s, acc_s = lax.fori_loop(0, (q0 + tq - 1) // tk + 1, sel_body, init)
    o_s = acc_s / l_s

    def win_body(kt, carry):
        k0 = pl.multiple_of(kt * tq, tq)
        kp, tt = k0 + _iota((tq, tq), 1), q0 + _iota((tq, tq), 0)
        allow = (kp <= tt) & (kp > tt - WINDOW)
        return _flash_step(q4, kw_ref[0, pl.ds(k0, tq), :], vw_ref[0, pl.ds(k0, tq), :], allow, carry, tq=tq)

    _, l_w, acc_w = lax.fori_loop(jnp.maximum(qi - WINDOW // tq, 0), qi + 1, win_body, init)
    o_w = acc_w / l_w

    gate = gate_ref[0]
    ys = []
    for g in range(NSA_GROUP):
        y = (gate[:, 3 * g:3 * g + 1] * oc_ref[0, :, g * LANES:(g + 1) * LANES]
             + gate[:, 3 * g + 1:3 * g + 2] * o_s[g] + gate[:, 3 * g + 2:3 * g + 3] * o_w[g])
        ys.append(y)
    low = _iota((tq, LANES), 1) < HEAD_DIM
    first_kv = kv == 0
    for j in range(NSA_GROUP // 2):
        a, b2 = ys[2 * j], ys[2 * j + 1]
        lo_half = jnp.where(first_kv, a, pltpu.roll(a, HEAD_DIM, 1))
        hi_half = jnp.where(first_kv, pltpu.roll(b2, HEAD_DIM, 1), b2)
        y_ref[0, :, j * LANES:(j + 1) * LANES] = jnp.where(low, lo_half, hi_half).astype(BF16)


def nsa_sel_win(q, ks, vs, kw, vw, sel, oc, gates, *, tq, tk):
    b, t, nq = q.shape
    hk = nq // (NSA_GROUP * LANES)
    n_slc = t // SLC_BLOCK
    gq = NSA_GROUP * LANES
    full = pl.BlockSpec((1, t, LANES), lambda i, k, j: (i, 0, 0))
    qspec = pl.BlockSpec((1, tq, gq), lambda i, k, j: (i, j, k))
    return pl.pallas_call(
        functools.partial(_sel_win_kernel, tq=tq, tk=tk, n_slc=n_slc),
        grid=(b, hk, t // tq),
        in_specs=[qspec, full, full, full, full,
                  pl.BlockSpec((1, 1, tq, n_slc), lambda i, k, j: (i, k, j, 0)),
                  qspec,
                  pl.BlockSpec((1, tq, LANES), lambda i, k, j: (i, j, k))],
        out_specs=pl.BlockSpec((1, tq, NSA_GROUP * HEAD_DIM), lambda i, k, j: (i, j, k)),
        out_shape=jax.ShapeDtypeStruct((b, t, hk * NSA_GROUP * HEAD_DIM), BF16),
        compiler_params=_cp(3),
    )(q, ks, vs, kw, vw, sel, oc, gates)


def _gmlp_kernel(u_ref, v_ref, g_ref, b_ref, avg_ref, ws_ref, bs_ref, y_ref, *, tg):
    u = _gelu(u_ref[0])
    v = _gelu(v_ref[0])
    avg = avg_ref[...]

    def gmean(z):
        hi, lo = _split2(z)
        return _dot(hi, avg) + _dot(lo, avg)

    d = v - gmean(v)
    vn = (d * lax.rsqrt(gmean(d * d) + EPS) * g_ref[...] + b_ref[...]).astype(BF16)
    c = GMLP_CHUNK
    causal = _iota((c, c), 0) >= _iota((c, c), 1)
    low = _iota((c, LANES), 1) < HEAD_DIM
    width = u.shape[1]
    for j in range(width // LANES):
        w0 = jnp.where(causal, ws_ref[2 * j], 0.0).astype(BF16)
        w1 = jnp.where(causal, ws_ref[2 * j + 1], 0.0).astype(BF16)
        cols = slice(j * LANES, (j + 1) * LANES)
        for ci in range(tg // c):
            rows = slice(ci * c, (ci + 1) * c)
            v2 = vn[rows, cols]
            mixed = jnp.where(low, _dot(w0, v2), _dot(w1, v2)) + bs_ref[:, cols]
            y_ref[0, rows, cols] = (u[rows, cols] * mixed).astype(BF16)


def gmlp_gating(uv, ln_g, ln_b, ws, bs, *, tg):
    b, t, w2 = uv.shape
    w = w2 // 2
    ngrp = w // HEAD_DIM
    avg = jnp.asarray(np.kron(np.eye(ngrp), np.full((HEAD_DIM, HEAD_DIM), 1.0 / HEAD_DIM)), BF16)
    bs_exp = jnp.repeat(bs.T, HEAD_DIM, axis=1)
    return pl.pallas_call(
        functools.partial(_gmlp_kernel, tg=tg),
        grid=(b, t // tg),
        in_specs=[pl.BlockSpec((1, tg, w), lambda i, j: (i, j, 0)),
                  pl.BlockSpec((1, tg, w), lambda i, j: (i, j, 1)),
                  _const_spec((1, w)), _const_spec((1, w)), _const_spec((w, w)),
                  _const_spec(ws.shape), _const_spec((GMLP_CHUNK, w))],
        out_specs=pl.BlockSpec((1, tg, w), lambda i, j: (i, j, 0)),
        out_shape=jax.ShapeDtypeStruct((b, t, w), BF16),
        compiler_params=_cp(2),
    )(uv, uv, ln_g.reshape(1, w), ln_b.reshape(1, w), avg, ws, bs_exp)


def _outproj_ffn_kernel(ya_ref, yb_ref, x_ref, wo_ref, g1_ref, n2_ref, sh_ref, sc_ref, g2_ref,
                        wup_ref, cw_ref, cb_ref, wdn_ref, o_ref, carry_ref, *, tm, nch):
    @pl.when(pl.program_id(1) == 0)
    def _():
        carry_ref[...] = jnp.zeros(carry_ref.shape, F32)

    half = ya_ref.shape[2]
    y = _dot(ya_ref[0], wo_ref[0:half, :]) + _dot(yb_ref[0], wo_ref[half:2 * half, :])
    x1 = x_ref[0] + g1_ref[0] * y
    h = _norm_mod(x1, n2_ref[...], sh_ref[0], sc_ref[0]).astype(BF16)
    cwid = wup_ref.shape[2]
    row = _iota((tm, cwid), 0)

    def conv_half(idx):
        up = _dot(h, wup_ref[idx])
        prev = carry_ref[idx]
        r1 = jnp.where(row == 0, prev[7:8, :], pltpu.roll(up, 1, 0))
        r2 = jnp.where(row == 0, prev[6:7, :], jnp.where(row == 1, prev[7:8, :], pltpu.roll(up, 2, 0)))
        carry_ref[idx] = up[tm - 8:tm, :]
        w = cw_ref[idx]
        return w[0:1, :] * r2 + w[1:2, :] * r1 + w[2:3, :] * up + cb_ref[idx]

    def chunk(c, acc):
        a = conv_half(c)
        b2 = conv_half(nch + c)
        act = (a * _sigmoid(a) * b2).astype(BF16)
        return acc + _dot(act, wdn_ref[c])

    acc = lax.fori_loop(0, nch, chunk, jnp.zeros(x1.shape, F32))
    o_ref[0] = x1 + g2_ref[0] * acc


def outproj_ffn(ya, yb, x, wo, g1, n2g, sh2, sc2, g2, wup, cw, cb, wdn, *, tm):
    b, t, d = x.shape
    half = ya.shape[2]
    nch = wdn.shape[0]
    cwid = wup.shape[2]
    row = lambda wdt: pl.BlockSpec((1, tm, wdt), lambda i, j: (i, j, 0))
    vec = pl.BlockSpec((1, 1, d), lambda i, j: (i, 0, 0))
    return pl.pallas_call(
        functools.partial(_outproj_ffn_kernel, tm=tm, nch=nch),
        grid=(b, t // tm),
        in_specs=[row(half), row(half), row(d), _const_spec(wo.shape), vec, _const_spec((1, d)), vec, vec, vec,
                  _const_spec(wup.shape), _const_spec(cw.shape), _const_spec(cb.shape), _const_spec(wdn.shape)],
        out_specs=row(d),
        out_shape=jax.ShapeDtypeStruct((b, t, d), F32),
        scratch_shapes=[pltpu.VMEM((2 * nch, 8, cwid), F32)],
        compiler_params=_cp(2),
    )(ya, yb, x, wo, g1, n2g, sh2, sc2, g2, wup, cw, cb, wdn)


def _odd_proj_kernel(x_ref, g_ref, sh_ref, sc_ref, w_ref, mu_ref, w0_ref, wb_ref, a0_ref, ab_ref, gb_ref,
                     xi_ref, al_ref, rho_ref, ones_ref, tril_ref, tot_ref,
                     rt_ref, kt_ref, bt_ref, ktt_ref, v_ref, ktp_ref, bp_ref, pc_ref, gout_ref, bonus_ref,
                     gd_ref, xd_ref, carry_ref, *, tm, wmix, nc):
    @pl.when(pl.program_id(1) == 0)
    def _():
        carry_ref[...] = jnp.zeros(carry_ref.shape, F32)

    h = _norm_mod(x_ref[0], g_ref[...], sh_ref[0], sc_ref[0]).astype(BF16)
    pc = _dot(h, w_ref[:, 0:nc])
    pd = _dot(h, w_ref[:, nc:nc + 2 * wmix])
    gd_ref[0] = pd[:, 0:wmix]
    xd_ref[0] = pd[:, wmix:2 * wmix]

    row = _iota((tm, nc), 0)
    prev = jnp.where(row == 0, carry_ref[0:1, :], pltpu.roll(pc, 1, 0))
    carry_ref[0:1, :] = pc[tm - 1:tm, :]
    pc = pc + mu_ref[...] * (prev - pc)

    r, k, v = pc[:, 0:wmix], pc[:, wmix:2 * wmix], pc[:, 2 * wmix:3 * wmix]
    wa = pc[:, 3 * wmix:3 * wmix + LANES]
    gl = pc[:, 3 * wmix + LANES:3 * wmix + 2 * LANES]
    log_w = -math.exp(-0.5) * _sigmoid(w0_ref[...] + _dot(jnp.tanh(wa).astype(BF16), wb_ref[...]))
    a = _sigmoid(a0_ref[...] + _dot(wa.astype(BF16), ab_ref[...]))
    gout_ref[0] = _dot(_sigmoid(gl).astype(BF16), gb_ref[...])

    ones = ones_ref[...]

    def gsum(z):
        hi, lo = _split2(z)
        return _dot(hi, ones) + _dot(lo, ones)

    kap = k * xi_ref[...]
    kap = kap * lax.rsqrt(gsum(kap * kap) + EPS)
    kt = k * (1.0 + (a - 1.0) * al_ref[...])
    bonus_ref[0] = gsum(r * rho_ref[...] * kt) * v
    bvec = a * kap

    cs = _lhs_exact_dot(tril_ref[...], log_w)
    tot = _lhs_exact_dot(tot_ref[...], log_w)
    dec_out = jnp.exp(-cs)
    dec_end = jnp.exp(tot - cs)
    rt_ref[0] = (r * jnp.exp(cs)).astype(BF16)
    kt_ref[0] = (kap * jnp.exp(cs - log_w)).astype(BF16)
    bt_ref[0] = (bvec * dec_out).astype(BF16)
    ktt_ref[0] = (kt * dec_out).astype(BF16)
    v_ref[0] = v.astype(BF16)
    ktp_ref[0] = (kt * dec_end).astype(BF16)
    bp_ref[0] = (bvec * dec_end).astype(BF16)
    pc_ref[0] = jnp.exp(tot)


def odd_proj(x, g, shift, scale, w, mu, w0, wb, a0, ab, gb, xi, alpha, rho, *, tm):
    b, t, d = x.shape
    wmix = w0.shape[1]
    nc = mu.shape[1]
    nh = wmix // HEAD_DIM
    ones = jnp.asarray(np.kron(np.eye(nh), np.ones((HEAD_DIM, HEAD_DIM))), BF16)
    nchunk = tm // RWKV_CHUNK
    tril = jnp.asarray(np.kron(np.eye(nchunk), np.tril(np.ones((RWKV_CHUNK, RWKV_CHUNK)))), BF16)
    tot = jnp.asarray(np.kron(np.eye(nchunk), np.ones((RWKV_CHUNK, RWKV_CHUNK))), BF16)
    row = lambda wdt: pl.BlockSpec((1, tm, wdt), lambda i, j: (i, j, 0))
    vec = pl.BlockSpec((1, 1, d), lambda i, j: (i, 0, 0))
    cvec = _const_spec((1, wmix))
    outs = [BF16] * 7 + [F32] * 5
    return pl.pallas_call(
        functools.partial(_odd_proj_kernel, tm=tm, wmix=wmix, nc=nc),
        grid=(b, t // tm),
        in_specs=[row(d), _const_spec((1, d)), vec, vec, _const_spec(w.shape), _const_spec((1, nc)),
                  cvec, _const_spec(wb.shape), cvec, _const_spec(ab.shape), _const_spec(gb.shape),
                  cvec, cvec, cvec, _const_spec(ones.shape), _const_spec(tril.shape), _const_spec(tot.shape)],
        out_specs=[row(wmix)] * 12,
        out_shape=[jax.ShapeDtypeStruct((b, t, wmix), dt) for dt in outs],
        scratch_shapes=[pltpu.VMEM((8, nc), F32)],
        compiler_params=_cp(2),
    )(x, g, shift, scale, w, mu, w0, wb, a0, ab, gb, xi, alpha, rho, ones, tril, tot)


def _rwkv_scan_kernel(rt_ref, kt_ref, bt_ref, ktt_ref, v_ref, ktp_ref, bp_ref, pc_ref, g_ref, bonus_ref,
                      lng_ref, lnb_ref, y_ref, s_ref, *, nh):
    @pl.when(pl.program_id(1) == 0)
    def _():
        s_ref[...] = jnp.zeros(s_ref.shape, F32)

    c = RWKV_CHUNK
    ti, si = _iota((c, c), 0), _iota((c, c), 1)
    strict, incl = ti > si, ti >= si
    eye = jnp.where(ti == si, 1.0, 0.0)
    for h in range(nh):
        cols = slice(h * HEAD_DIM, (h + 1) * HEAD_DIM)
        rt, kt, bt, ktt = rt_ref[0, :, cols], kt_ref[0, :, cols], bt_ref[0, :, cols], ktt_ref[0, :, cols]
        v, ktp, bp = v_ref[0, :, cols], ktp_ref[0, :, cols], bp_ref[0, :, cols]
        s = s_ref[h]
        sb = s.astype(BF16)
        n = jnp.where(strict, _dot_nt(kt, bt), 0.0)
        ak = jnp.where(strict, _dot_nt(kt, ktt), 0.0)
        qb = jnp.where(incl, _dot_nt(rt, bt), 0.0)
        qk = jnp.where(incl, _dot_nt(rt, ktt), 0.0)
        tinv = eye - n
        npow = n
        steps = int(math.log2(c)) - 1
        for it in range(steps):
            npb = npow.astype(BF16)
            npow = _dot(npb, npb)
            tinv = tinv + _dot(tinv.astype(BF16), npow.astype(BF16))
        rhs = _dot_nt(kt, sb) + _dot(ak.astype(BF16), v)
        u = _dot(tinv.astype(BF16), rhs.astype(BF16))
        ub = u.astype(BF16)
        y = _dot_nt(rt, sb) + _dot(qk.astype(BF16), v) - _dot(qb.astype(BF16), ub)
        s_ref[h] = s * pc_ref[0, 0:1, cols] + _dot_tn(v, ktp) - _dot_tn(ub, bp)
        mu = jnp.mean(y, axis=1, keepdims=True)
        var = jnp.mean(jnp.square(y - mu), axis=1, keepdims=True)
        yn = (y - mu) * lax.rsqrt(var + EPS) * lng_ref[:, cols] + lnb_ref[:, cols]
        y_ref[0, :, cols] = (g_ref[0, :, cols] * (yn + bonus_ref[0, :, cols])).astype(BF16)


def rwkv_scan(rt, kt, bt, ktt, v, ktp, bp, pc, g, bonus, ln_g, ln_b):
    b, t, wmix = rt.shape
    nh = wmix // HEAD_DIM
    c = RWKV_CHUNK
    row = pl.BlockSpec((1, c, wmix), lambda i, j: (i, j, 0))
    cvec = _const_spec((1, wmix))
    return pl.pallas_call(
        functools.partial(_rwkv_scan_kernel, nh=nh),
        grid=(b, t // c),
        in_specs=[row] * 10 + [cvec, cvec],
        out_specs=row,
        out_shape=jax.ShapeDtypeStruct((b, t, wmix), BF16),
        scratch_shapes=[pltpu.VMEM((nh, HEAD_DIM, HEAD_DIM), F32)],
        compiler_params=_cp(2),
    )(rt, kt, bt, ktt, v, ktp, bp, pc, g, bonus, ln_g, ln_b)


def _lru_kernel(gd_ref, xd_ref, cw_ref, cb_ref, wa_ref, ba_ref, wx_ref, bx_ref, lam_ref, y_ref,
                xcarry_ref, hcarry_ref, *, tl, kconv):
    @pl.when(pl.program_id(1) == 0)
    def _():
        xcarry_ref[...] = jnp.zeros(xcarry_ref.shape, F32)
        hcarry_ref[...] = jnp.zeros(hcarry_ref.shape, F32)

    x = xd_ref[0]
    wdt = x.shape[1]
    row = _iota((tl, wdt), 0)
    prev = xcarry_ref[...]
    xc = cw_ref[kconv - 1:kconv, :] * x + cb_ref[...]
    for dly in range(1, kconv):
        sh = pltpu.roll(x, dly, 0)
        for r0 in range(dly):
            sh = jnp.where(row == r0, prev[8 - dly + r0:8 - dly + r0 + 1, :], sh)
        xc = xc + cw_ref[kconv - 1 - dly:kconv - dly, :] * sh
    xcarry_ref[...] = x[tl - 8:tl, :]

    xb = xc.astype(BF16)
    r = _sigmoid(_dot(xb, wa_ref[...]) + ba_ref[...])
    i = _sigmoid(_dot(xb, wx_ref[...]) + bx_ref[...])
    nl = -lam_ref[...]
    softplus = jnp.maximum(nl, 0.0) + jnp.log1p(jnp.exp(-jnp.abs(nl)))
    log_a = -LRU_C * r * softplus
    a = jnp.exp(log_a)
    th = jnp.tanh(log_a)
    bterm = jnp.sqrt(-2.0 * th / (1.0 - th)) * (i * xc)

    d = 1
    while d < tl:
        keep = row >= d
        a_sh = jnp.where(keep, pltpu.roll(a, d, 0), 1.0)
        b_sh = jnp.where(keep, pltpu.roll(bterm, d, 0), 0.0)
        bterm = a * b_sh + bterm
        a = a * a_sh
        d *= 2
    hseq = bterm + a * hcarry_ref[0:1, :]
    hcarry_ref[0:1, :] = hseq[tl - 1:tl, :]
    y_ref[0] = (_gelu(gd_ref[0]) * hseq).astype(BF16)


def rglru(gd, xd, conv_w, conv_b, wa_bd, ba, wx_bd, bx, lam, *, tl):
    b, t, wdt = xd.shape
    kconv = conv_w.shape[0]
    row = pl.BlockSpec((1, tl, wdt), lambda i, j: (i, j, 0))
    cvec = _const_spec((1, wdt))
    return pl.pallas_call(
        functools.partial(_lru_kernel, tl=tl, kconv=kconv),
        grid=(b, t // tl),
        in_specs=[row, row, _const_spec(conv_w.shape), cvec, _const_spec(wa_bd.shape), cvec,
                  _const_spec(wx_bd.shape), cvec, cvec],
        out_specs=row,
        out_shape=jax.ShapeDtypeStruct((b, t, wdt), BF16),
        scratch_shapes=[pltpu.VMEM((8, wdt), F32), pltpu.VMEM((8, wdt), F32)],
        compiler_params=_cp(2),
    )(gd, xd, conv_w, conv_b, wa_bd, ba, wx_bd, bx, lam)


def _final_norm_kernel(x_ref, g_ref, o_ref):
    x = x_ref[0]
    o_ref[0] = x * lax.rsqrt(jnp.mean(x * x, axis=-1, keepdims=True) + EPS) * g_ref[...]


def final_norm(x, g, *, tm):
    b, t, d = x.shape
    return pl.pallas_call(
        _final_norm_kernel,
        grid=(b, t // tm),
        in_specs=[pl.BlockSpec((1, tm, d), lambda i, j: (i, j, 0)), _const_spec((1, d))],
        out_specs=pl.BlockSpec((1, tm, d), lambda i, j: (i, j, 0)),
        out_shape=jax.ShapeDtypeStruct((b, t, d), F32),
        compiler_params=_cp(2),
    )(x, g.reshape(1, d))


def _block_diag(w):
    n, a, b = w.shape
    eye = jnp.eye(n, dtype=w.dtype)
    return (eye[:, None, :, None] * w[:, :, None, :]).reshape(n * a, n * b)


def _even_weights(w_in, mix_half):
    d = w_in.shape[0]
    nh = mix_half // HEAD_DIM
    hk = nh // NSA_GROUP
    nkv = hk * HEAD_DIM
    o = 0
    wq = w_in[:, o:o + mix_half].reshape(d, nh, HEAD_DIM); o += mix_half
    kvs = []
    for _ in range(6):
        kvs.append(w_in[:, o:o + nkv]); o += nkv
    wg = w_in[:, o:o + 3 * nh].reshape(d, hk, 3 * NSA_GROUP); o += 3 * nh
    wuv = w_in[:, o:]
    kv_of = jnp.arange(nh) // NSA_GROUP
    place = jax.nn.one_hot(kv_of, LANES // HEAD_DIM, dtype=w_in.dtype)
    wq_pad = (wq[:, :, None, :] * place[None, :, :, None]).reshape(d, nh * LANES)
    wg_pad = jnp.pad(wg, ((0, 0), (0, 0), (0, LANES - 3 * NSA_GROUP))).reshape(d, hk * LANES)
    return jnp.concatenate([wq_pad] + kvs + [wg_pad, wuv], axis=1).astype(BF16)


def _compress_weights(pe, w1, w2, hk):
    hid = w1.shape[1]
    pe2 = jnp.tile(pe, (1, hk))
    w1r = w1.reshape(CMP_BLOCK, HEAD_DIM, hid)
    eye = jnp.eye(hk, dtype=w1.dtype)
    w1_bd = (eye[None, :, None, :, None] * w1r[:, None, :, None, :]).reshape(CMP_BLOCK, hk * HEAD_DIM, hk * hid)
    w2_bd = (eye[:, None, :, None] * w2[None, :, None, :]).reshape(hk * hid, hk * HEAD_DIM)
    return pe2, w1_bd.astype(BF16), w2_bd.astype(BF16)


def _ffn_weights(w_up, conv_w, conv_b, w_down, cwid):
    d, f2 = w_up.shape
    n2 = f2 // cwid
    wup = w_up.reshape(d, n2, cwid).transpose(1, 0, 2).astype(BF16)
    cw = conv_w.reshape(conv_w.shape[0], n2, cwid).transpose(1, 0, 2)
    cb = conv_b.reshape(n2, 1, cwid)
    wdn = w_down.reshape(n2 // 2, cwid, w_down.shape[1]).astype(BF16)
    return wup, cw, cb, wdn


def kernel(x, c, positions, ada_w, ada_b, norm1_g, norm2_g, ev_w_in, ev_w_out, nsa_pe_k, nsa_w1_k, nsa_w2_k, nsa_pe_v, nsa_w1_v, nsa_w2_v, gmlp_ln_g, gmlp_ln_b, gmlp_ws, gmlp_bs, od_w_in, od_w_out, rwkv_mu, rwkv_w0, rwkv_wB, rwkv_a0, rwkv_aB, rwkv_gB, rwkv_xi, rwkv_alpha, rwkv_rho, rwkv_ln_g, rwkv_ln_b, lru_conv_w, lru_conv_b, lru_wa, lru_ba, lru_wx, lru_bx, lru_lambda, ffn_up, ffn_conv_w, ffn_conv_b, ffn_down, final_g):
    b, t, d = x.shape
    depth = ada_w.shape[0]
    mix_half = d // 2
    hk = mix_half // HEAD_DIM // NSA_GROUP
    assert hk * HEAD_DIM == LANES and t % 512 == 0
    tm = 512

    mod = adaln_mod(c, ada_w, ada_b)
    mods = mod.reshape(depth, b, 6, 1, d)
    tabs = rope_tables(positions, tm)
    n_cmp = (t - CMP_BLOCK) // CMP_STRIDE + 1
    ngrp = t // CMP_STRIDE
    pos_end = jnp.pad(positions[:, CMP_BLOCK - 1::CMP_STRIDE][:, :n_cmp], ((0, 0), (0, ngrp - n_cmp)))
    end_tabs = rope_tables(pos_end, ngrp)

    for layer in range(depth):
        i = layer // 2
        sh1, sc1, g1, sh2, sc2, g2 = (mods[layer, :, j] for j in range(6))
        n1 = norm1_g[layer].reshape(1, d)
        n2 = norm2_g[layer].reshape(1, d)
        if layer % 2 == 0:
            w_in = _even_weights(ev_w_in[i], mix_half)
            q, kc, vc, ks, vs, kw, vw, gates, uv = even_proj(x, n1, sh1, sc1, w_in, tabs, tm=tm)
            pek, w1k, w2k = _compress_weights(nsa_pe_k[i], nsa_w1_k[i], nsa_w2_k[i], hk)
            pev, w1v, w2v = _compress_weights(nsa_pe_v[i], nsa_w1_v[i], nsa_w2_v[i], hk)
            kcmp, vcmp = nsa_compress(kc, vc, pek, w1k, w2k, pev, w1v, w2v, end_tabs)
            oc, sel_t = nsa_cmp_topk(q, kcmp, vcmp, tq=256)
            sel = jnp.swapaxes(sel_t, 2, 3).astype(BF16)
            ya = nsa_sel_win(q, ks, vs, kw, vw, sel, oc, gates, tq=128, tk=256)
            yb = gmlp_gating(uv, gmlp_ln_g[i], gmlp_ln_b[i], gmlp_ws[i], gmlp_bs[i], tg=tm)
            wo = ev_w_out[i].astype(BF16)
        else:
            w_in = od_w_in[i].astype(BF16)
            lora = rwkv_wB.shape[1]
            wb = jnp.pad(rwkv_wB[i], ((0, LANES - lora), (0, 0))).astype(BF16)
            ab = jnp.pad(rwkv_aB[i], ((LANES - rwkv_aB.shape[1], 0), (0, 0))).astype(BF16)
            v1 = lambda p: p.reshape(1, -1)
            (rt, kt, bt, ktt, v, ktp, bp, pc, gout, bonus, gd, xd) = odd_proj(
                x, n1, sh1, sc1, w_in, v1(rwkv_mu[i]), v1(rwkv_w0[i]), wb, v1(rwkv_a0[i]), ab,
                rwkv_gB[i].astype(BF16), v1(rwkv_xi[i]), v1(rwkv_alpha[i]), v1(rwkv_rho[i]), tm=tm)
            ya = rwkv_scan(rt, kt, bt, ktt, v, ktp, bp, pc, gout, bonus, v1(rwkv_ln_g[i]), v1(rwkv_ln_b[i]))
            yb = rglru(gd, xd, lru_conv_w[i], v1(lru_conv_b[i]), _block_diag(lru_wa[i]).astype(BF16),
                       v1(lru_ba[i]), _block_diag(lru_wx[i]).astype(BF16), v1(lru_bx[i]), v1(lru_lambda[i]), tl=tm)
            wo = od_w_out[i].astype(BF16)
        wup, cw, cb, wdn = _ffn_weights(ffn_up[layer], ffn_conv_w[layer], ffn_conv_b[layer], ffn_down[layer], 256)
        x = outproj_ffn(ya, yb, x, wo, g1, n2, sh2, sc2, g2, wup, cw, cb, wdn, tm=tm)
    return final_norm(x, final_g, tm=tm)
```

```python
import functools
import math

import numpy as np
import jax
import jax.numpy as jnp
from jax import lax
from jax.experimental import pallas as pl
from jax.experimental.pallas import tpu as pltpu

F32 = jnp.float32
BF16 = jnp.bfloat16

HEAD_DIM = 64
ROPE_DIM = HEAD_DIM // 4
ROPE_THETA = 500000.0
EPS = 1e-6
NEG_INF = -1e30
LANES = 128

NSA_GROUP = 4
CMP_BLOCK = 32
CMP_STRIDE = 16
SLC_BLOCK = 64
SLC_SHIFT = 6
SLC_TOPK = 16
WINDOW = 512
FORCE_SCORE = 1e3
GMLP_CHUNK = 128
RWKV_CHUNK = 64
LRU_C = 8.0

VMEM_LIMIT = 56 * 1024 * 1024


def _cp(n_axes, vmem=VMEM_LIMIT):
    return pltpu.CompilerParams(dimension_semantics=("arbitrary",) * n_axes, vmem_limit_bytes=vmem)


def _const_spec(shape):
    nd = len(shape)
    return pl.BlockSpec(shape, lambda *_: (0,) * nd, pipeline_mode=pl.Buffered(1))


def _dot(a, b):
    return jnp.dot(a, b, preferred_element_type=F32)


def _dot_nt(a, b):
    return lax.dot_general(a, b, (((1,), (1,)), ((), ())), preferred_element_type=F32)


def _dot_tn(a, b):
    return lax.dot_general(a, b, (((0,), (0,)), ((), ())), preferred_element_type=F32)


def _split2(z):
    hi = z.astype(BF16)
    lo = (z - hi.astype(F32)).astype(BF16)
    return hi, lo


def _split3(z):
    hi = z.astype(BF16)
    r = z - hi.astype(F32)
    mid = r.astype(BF16)
    lo = (r - mid.astype(F32)).astype(BF16)
    return hi, mid, lo


def _dot_exact_rhs(z, m):
    hi, mid, lo = _split3(z)
    return _dot(hi, m) + _dot(mid, m) + _dot(lo, m)


def _lhs_exact_dot(m, z):
    hi, mid, lo = _split3(z)
    return _dot(m, hi) + _dot(m, mid) + _dot(m, lo)


def _norm_mod(x, g, shift, scale):
    y = x * lax.rsqrt(jnp.mean(x * x, axis=-1, keepdims=True) + EPS)
    return (y * g) * (1.0 + scale) + shift


def _gelu(x):
    return jax.nn.gelu(x, approximate=True)


def _sigmoid(x):
    return jax.nn.sigmoid(x)


def _iota(shape, axis):
    return lax.broadcasted_iota(jnp.int32, shape, axis)


def _adaln_kernel(c_ref, w_ref, b_ref, o_ref):
    cond = c_ref[...]
    cond = cond * _sigmoid(cond)
    o_ref[0] = _dot(cond.astype(BF16), w_ref[0].astype(BF16)) + b_ref[0]


def adaln_mod(c, ada_w, ada_b):
    depth, d, n = ada_w.shape
    b = c.shape[0]
    rows = 16
    cp = jnp.zeros((rows, d), F32).at[:b].set(c)
    tn = 1536
    out = pl.pallas_call(
        _adaln_kernel,
        grid=(depth, n // tn),
        in_specs=[pl.BlockSpec((rows, d), lambda l, j: (0, 0)),
                  pl.BlockSpec((1, d, tn), lambda l, j: (l, 0, j)),
                  pl.BlockSpec((1, 1, tn), lambda l, j: (l, 0, j))],
        out_specs=pl.BlockSpec((1, rows, tn), lambda l, j: (l, 0, j)),
        out_shape=jax.ShapeDtypeStruct((depth, rows, n), F32),
        compiler_params=_cp(2),
    )(cp, ada_w, ada_b.reshape(depth, 1, n))
    return out[:, :b]


def _rope_kernel(pos_ref, inv_ref, c_ref, sa_ref, sb_ref):
    ang = pos_ref[0].astype(F32) * inv_ref[...]
    cos, sin = jnp.cos(ang), jnp.sin(ang)
    d = _iota(ang.shape, 1) % HEAD_DIM
    c_ref[0] = jnp.where(d < ROPE_DIM, cos, 1.0)
    sa_ref[0] = jnp.where(d < ROPE_DIM // 2, -sin, 0.0)
    sb_ref[0] = jnp.where((d >= ROPE_DIM // 2) & (d < ROPE_DIM), sin, 0.0)


def rope_tables(pos, tm):
    b, n = pos.shape
    half = ROPE_DIM // 2
    inv = ROPE_THETA ** (-jnp.arange(0, ROPE_DIM, 2, dtype=F32) / ROPE_DIM)
    inv_lane = jnp.tile(inv, LANES // half).reshape(1, LANES)
    spec = pl.BlockSpec((1, tm, LANES), lambda i, j: (i, j, 0))
    return pl.pallas_call(
        _rope_kernel,
        grid=(b, n // tm),
        in_specs=[pl.BlockSpec((1, tm, 1), lambda i, j: (i, j, 0)),
                  pl.BlockSpec((1, LANES), lambda i, j: (0, 0))],
        out_specs=[spec, spec, spec],
        out_shape=[jax.ShapeDtypeStruct((b, n, LANES), F32)] * 3,
        compiler_params=_cp(2),
    )(pos[..., None], inv_lane)


def _rope(y, c, sa, sb):
    return y * c + pltpu.roll(y, LANES - ROPE_DIM // 2, 1) * sa + pltpu.roll(y, ROPE_DIM // 2, 1) * sb


def _even_proj_kernel(x_ref, g_ref, sh_ref, sc_ref, w_ref, c_ref, sa_ref, sb_ref,
                      q_ref, kc_ref, vc_ref, ks_ref, vst_ref, kw_ref, vwt_ref, gate_ref, uv_ref,
                      *, nq, nuv):
    h = _norm_mod(x_ref[0], g_ref[...], sh_ref[0], sc_ref[0]).astype(BF16)
    c, sa, sb = c_ref[0], sa_ref[0], sb_ref[0]
    scale = HEAD_DIM ** -0.5
    yq = _dot(h, w_ref[:, 0:nq])
    for j in range(nq // LANES):
        sl = slice(j * LANES, (j + 1) * LANES)
        q_ref[0, :, sl] = (_rope(yq[:, sl], c, sa, sb) * scale).astype(BF16)
    o = nq
    ykv = _dot(h, w_ref[:, o:o + 6 * LANES])
    kc_ref[0] = ykv[:, 0:LANES]
    vc_ref[0] = ykv[:, LANES:2 * LANES]
    ks_ref[0] = _rope(ykv[:, 2 * LANES:3 * LANES], c, sa, sb).astype(BF16)
    kw_ref[0] = _rope(ykv[:, 4 * LANES:5 * LANES], c, sa, sb).astype(BF16)
    for ref, lo in ((vst_ref, 3 * LANES), (vwt_ref, 5 * LANES)):
        tile = ref.shape[3]
        for i in range(ref.shape[1]):
            ref[0, i] = ykv[i * tile:(i + 1) * tile, lo:lo + LANES].T.astype(BF16)
    o += 6 * LANES
    gate_ref[0] = _sigmoid(_dot(h, w_ref[:, o:o + 2 * LANES]))
    o += 2 * LANES
    uv_ref[0] = _dot(h, w_ref[:, o:o + nuv])


def even_proj(x, g, shift, scale, w, tabs, *, tm, tk_sel, tk_win):
    b, t, d = x.shape
    n = w.shape[1]
    nq = 2 * NSA_GROUP * LANES
    nuv = n - nq - 8 * LANES
    row = lambda wdt: pl.BlockSpec((1, tm, wdt), lambda i, j: (i, j, 0))
    vec = pl.BlockSpec((1, 1, d), lambda i, j: (i, 0, 0))
    rows = lambda wdt, dt: (row(wdt), jax.ShapeDtypeStruct((b, t, wdt), dt))
    vt = lambda tk: (pl.BlockSpec((1, tm // tk, LANES, tk), lambda i, j: (i, j, 0, 0)),
                     jax.ShapeDtypeStruct((b, t // tk, LANES, tk), BF16))
    outs = [rows(nq, BF16), rows(LANES, F32), rows(LANES, F32), rows(LANES, BF16), vt(tk_sel),
            rows(LANES, BF16), vt(tk_win), rows(2 * LANES, F32), rows(nuv, F32)]
    return pl.pallas_call(
        functools.partial(_even_proj_kernel, nq=nq, nuv=nuv),
        grid=(b, t // tm),
        in_specs=[row(d), _const_spec((1, d)), vec, vec, _const_spec((d, n)),
                  row(LANES), row(LANES), row(LANES)],
        out_specs=[s for s, _ in outs],
        out_shape=[s for _, s in outs],
        compiler_params=_cp(2),
    )(x, g, shift, scale, w, *tabs)


def _compress_kernel(kc_ref, vc_ref, pek_ref, w1k_ref, w2k_ref, pev_ref, w1v_ref, w2v_ref,
                     c_ref, sa_ref, sb_ref, ko_ref, vo_ref, *, ngrp, n_cmp):
    half = CMP_BLOCK // 2

    def phi(src_ref, pe_ref, w1_ref, w2_ref):
        p = jnp.zeros((ngrp, w1_ref.shape[2]), F32)
        q = jnp.zeros((ngrp, w1_ref.shape[2]), F32)
        for l in range(half):
            a = src_ref[0, pl.ds(l, ngrp, stride=CMP_STRIDE), :]
            p = p + _dot((a + pe_ref[l:l + 1, :]).astype(BF16), w1_ref[l])
            q = q + _dot((a + pe_ref[half + l:half + l + 1, :]).astype(BF16), w1_ref[half + l])
        hid = p + pltpu.roll(q, ngrp - 1, 0)
        return _dot(_gelu(hid).astype(BF16), w2_ref[...])

    row = _iota((ngrp, LANES), 0)
    kc = _rope(phi(kc_ref, pek_ref, w1k_ref, w2k_ref), c_ref[0], sa_ref[0], sb_ref[0])
    ko_ref[0] = jnp.where(row < n_cmp, kc, 0.0).astype(BF16)
    vc = phi(vc_ref, pev_ref, w1v_ref, w2v_ref)
    vo_ref[0] = jnp.where(row < n_cmp, vc, 0.0).astype(BF16)


def nsa_compress(kc, vc, pek, w1k, w2k, pev, w1v, w2v, end_tabs):
    b, t, _ = kc.shape
    ngrp = t // CMP_STRIDE
    n_cmp = (t - CMP_BLOCK) // CMP_STRIDE + 1
    tok = pl.BlockSpec((1, t, LANES), lambda i: (i, 0, 0))
    grp = pl.BlockSpec((1, ngrp, LANES), lambda i: (i, 0, 0))
    return pl.pallas_call(
        functools.partial(_compress_kernel, ngrp=ngrp, n_cmp=n_cmp),
        grid=(b,),
        in_specs=[tok, tok, _const_spec(pek.shape), _const_spec(w1k.shape), _const_spec(w2k.shape),
                  _const_spec(pev.shape), _const_spec(w1v.shape), _const_spec(w2v.shape), grp, grp, grp],
        out_specs=[grp, grp],
        out_shape=[jax.ShapeDtypeStruct((b, ngrp, LANES), BF16)] * 2,
        compiler_params=_cp(1),
    )(kc, vc, pek, w1k, w2k, pev, w1v, w2v, *end_tabs)


def _cmp_topk_kernel(q_ref, kc_ref, vc_ref, ovt_ref, oc_ref, sel_ref, *, tq, n_cmp, ncp, n_slc, k_top):
    t0 = pl.program_id(2) * tq
    kc, vc = kc_ref[0], vc_ref[0]
    end = CMP_BLOCK - 1
    n_col, t_row = _iota((tq, ncp), 1), t0 + _iota((tq, ncp), 0)
    valid = (n_col * CMP_STRIDE + end <= t_row) & (n_col < n_cmp)
    n_row, t_col = _iota((ncp, tq), 0), t0 + _iota((ncp, tq), 1)
    valid_t = (n_row * CMP_STRIDE + end <= t_col) & (n_row < n_cmp)
    psum_t = jnp.zeros((ncp, tq), F32)
    for g in range(NSA_GROUP):
        qg = q_ref[0, :, g * LANES:(g + 1) * LANES]
        s = jnp.where(valid, _dot_nt(qg, kc), NEG_INF)
        e = jnp.where(valid, jnp.exp(s - jnp.max(s, axis=1, keepdims=True)), 0.0)
        l = jnp.sum(e, axis=1, keepdims=True)
        p = e / jnp.where(l > 0.0, l, 1.0)
        oc_ref[0, :, g * LANES:(g + 1) * LANES] = _dot(p.astype(BF16), vc)
        st = jnp.where(valid_t, _dot_nt(kc, qg), NEG_INF)
        et = jnp.where(valid_t, jnp.exp(st - jnp.max(st, axis=0, keepdims=True)), 0.0)
        lt = jnp.sum(et, axis=0, keepdims=True)
        psum_t = psum_t + et / jnp.where(lt > 0.0, lt, 1.0)
    hi, lo = _split2(psum_t)
    imp = _dot(ovt_ref[...], hi) + _dot(ovt_ref[...], lo)
    j = _iota((n_slc, tq), 0)
    t = t0 + _iota((n_slc, tq), 1)
    cur = jnp.right_shift(t, SLC_SHIFT)
    forced = (j == 0) | (j == cur) | (j == cur - 1)
    score = jnp.where(forced, FORCE_SCORE, jnp.where(j * SLC_BLOCK <= t, imp, -1.0))
    rank = jnp.zeros((n_slc, tq), jnp.int32)
    for jj in range(n_slc):
        r = score[jj:jj + 1, :]
        beats = (r > score) | ((r == score) & (j > jj))
        rank = rank + beats.astype(jnp.int32)
    sel_ref[0, 0] = jnp.where(rank < k_top, 1.0, 0.0).astype(BF16)


def nsa_cmp_topk(q, kcmp, vcmp, *, tq):
    b, t, nq = q.shape
    hk = nq // (NSA_GROUP * LANES)
    ncp = kcmp.shape[1]
    n_cmp = (t - CMP_BLOCK) // CMP_STRIDE + 1
    n_slc = t // SLC_BLOCK
    cs = np.arange(ncp) * CMP_STRIDE
    ss = np.arange(n_slc) * SLC_BLOCK
    ov = np.clip(np.minimum(cs[:, None] + CMP_BLOCK, ss[None, :] + SLC_BLOCK)
                 - np.maximum(cs[:, None], ss[None, :]), 0, None).astype(np.float32) / CMP_BLOCK
    ov[n_cmp:] = 0.0
    ovt = jnp.asarray(ov.T, BF16)
    gq = NSA_GROUP * LANES
    return pl.pallas_call(
        functools.partial(_cmp_topk_kernel, tq=tq, n_cmp=n_cmp, ncp=ncp, n_slc=n_slc,
                          k_top=min(SLC_TOPK, n_slc)),
        grid=(b, hk, t // tq),
        in_specs=[pl.BlockSpec((1, tq, gq), lambda i, k, j: (i, j, k)),
                  pl.BlockSpec((1, ncp, LANES), lambda i, k, j: (i, 0, 0)),
                  pl.BlockSpec((1, ncp, LANES), lambda i, k, j: (i, 0, 0)),
                  _const_spec((n_slc, ncp))],
        out_specs=[pl.BlockSpec((1, tq, gq), lambda i, k, j: (i, j, k)),
                   pl.BlockSpec((1, 1, n_slc, tq), lambda i, k, j: (i, k, 0, j))],
        out_shape=[jax.ShapeDtypeStruct((b, t, nq), F32),
                   jax.ShapeDtypeStruct((b, hk, n_slc, t), BF16)],
        compiler_params=_cp(3),
    )(q, kcmp, vcmp, ovt)


def _flash_tile(q_ref, kb, vt, bias, carry):
    ss = [_dot_nt(kb, q_ref[0, :, g * LANES:(g + 1) * LANES]) + bias for g in range(NSA_GROUP)]
    out = []
    for g in range(NSA_GROUP):
        m, l, acc = carry[g]
        m_new = jnp.maximum(m, jnp.max(ss[g], axis=0, keepdims=True))
        alpha = jnp.exp(m - m_new)
        p = jnp.exp(ss[g] - m_new)
        l = alpha * l + jnp.sum(p, axis=0, keepdims=True)
        acc = alpha * acc + _dot(vt, p.astype(BF16))
        out.append((m_new, l, acc))
    return tuple(out)


def _sel_win_kernel(q_ref, ks_ref, vst_ref, kw_ref, vwt_ref, sel_ref, oc_ref, gate_ref, y_ref,
                    *, tq, tk, n_slc):
    kv = pl.program_id(1)
    qi = pl.program_id(2)
    q0 = qi * tq
    sel_t = sel_ref[0, 0]
    init = tuple((jnp.full((1, tq), NEG_INF, F32), jnp.zeros((1, tq), F32), jnp.zeros((LANES, tq), F32))
                 for _ in range(NSA_GROUP))

    def sel_tile(kt, carry, diagonal):
        k0 = pl.multiple_of(kt * tk, tk)
        expand = (jnp.right_shift(k0 + _iota((tk, n_slc), 0), SLC_SHIFT) == _iota((tk, n_slc), 1))
        allow = _dot(jnp.where(expand, 1.0, 0.0).astype(BF16), sel_t) > 0.5
        if diagonal:
            allow = allow & (k0 + _iota((tk, tq), 0) <= q0 + _iota((tk, tq), 1))
        bias = jnp.where(allow, 0.0, NEG_INF)
        return _flash_tile(q_ref, ks_ref[0, pl.ds(k0, tk), :], vst_ref[0, kt], bias, carry)

    kt_diag = q0 // tk
    carry = sel_tile(kt_diag, init, True)
    carry = lax.fori_loop(0, kt_diag, lambda kt, c: sel_tile(kt, c, False), carry)
    o_s = [(acc / l).T for _, l, acc in carry]

    nwt = WINDOW // tq + 1
    kt0 = jnp.maximum(qi - WINDOW // tq, 0)
    k0 = pl.multiple_of(kt0 * tq, tq)
    kp, tt = k0 + _iota((nwt * tq, tq), 0), q0 + _iota((nwt * tq, tq), 1)
    bias_w = jnp.where((kp <= tt) & (kp > tt - WINDOW), 0.0, NEG_INF)
    kwb = kw_ref[0, pl.ds(k0, nwt * tq), :]
    sw = [_dot_nt(kwb, q_ref[0, :, g * LANES:(g + 1) * LANES]) + bias_w for g in range(NSA_GROUP)]
    o_w = []
    for g in range(NSA_GROUP):
        p = jnp.exp(sw[g] - jnp.max(sw[g], axis=0, keepdims=True))
        l = jnp.sum(p, axis=0, keepdims=True)
        pb = p.astype(BF16)
        acc = _dot(vwt_ref[0, kt0], pb[0:tq])
        for i in range(1, nwt):
            acc = acc + _dot(vwt_ref[0, kt0 + i], pb[i * tq:(i + 1) * tq])
        o_w.append((acc / l).T)

    gate = gate_ref[0]
    ys = []
    for g in range(NSA_GROUP):
        y = (gate[:, 3 * g:3 * g + 1] * oc_ref[0, :, g * LANES:(g + 1) * LANES]
             + gate[:, 3 * g + 1:3 * g + 2] * o_s[g] + gate[:, 3 * g + 2:3 * g + 3] * o_w[g])
        ys.append(y)
    low = _iota((tq, LANES), 1) < HEAD_DIM
    first_kv = kv == 0
    for j in range(NSA_GROUP // 2):
        a, b2 = ys[2 * j], ys[2 * j + 1]
        lo_half = jnp.where(first_kv, a, pltpu.roll(a, HEAD_DIM, 1))
        hi_half = jnp.where(first_kv, pltpu.roll(b2, HEAD_DIM, 1), b2)
        y_ref[0, :, j * LANES:(j + 1) * LANES] = jnp.where(low, lo_half, hi_half).astype(BF16)


def nsa_sel_win(q, ks, vst, kw, vwt, sel_t, oc, gates, *, tq):
    b, t, nq = q.shape
    hk = nq // (NSA_GROUP * LANES)
    n_slc = t // SLC_BLOCK
    tk = vst.shape[3]
    assert vwt.shape[3] == tq and tk % tq == 0
    gq = NSA_GROUP * LANES
    full = pl.BlockSpec((1, t, LANES), lambda i, k, j: (i, 0, 0))
    tiles = lambda a: pl.BlockSpec((1,) + a.shape[1:], lambda i, k, j: (i, 0, 0, 0))
    qspec = pl.BlockSpec((1, tq, gq), lambda i, k, j: (i, j, k))
    return pl.pallas_call(
        functools.partial(_sel_win_kernel, tq=tq, tk=tk, n_slc=n_slc),
        grid=(b, hk, t // tq),
        in_specs=[qspec, full, tiles(vst), full, tiles(vwt),
                  pl.BlockSpec((1, 1, n_slc, tq), lambda i, k, j: (i, k, 0, j)),
                  qspec,
                  pl.BlockSpec((1, tq, LANES), lambda i, k, j: (i, j, k))],
        out_specs=pl.BlockSpec((1, tq, NSA_GROUP * HEAD_DIM), lambda i, k, j: (i, j, k)),
        out_shape=jax.ShapeDtypeStruct((b, t, hk * NSA_GROUP * HEAD_DIM), BF16),
        compiler_params=_cp(3),
    )(q, ks, vst, kw, vwt, sel_t, oc, gates)


def _gmlp_kernel(u_ref, v_ref, g_ref, b_ref, avg_ref, ws_ref, bs_ref, y_ref, *, tg):
    u = _gelu(u_ref[0])
    v = _gelu(v_ref[0])
    avg = avg_ref[...]

    def gmean(z):
        hi, lo = _split2(z)
        return _dot(hi, avg) + _dot(lo, avg)

    d = v - gmean(v)
    vn = (d * lax.rsqrt(gmean(d * d) + EPS) * g_ref[...] + b_ref[...]).astype(BF16)
    c = GMLP_CHUNK
    causal = _iota((c, c), 0) >= _iota((c, c), 1)
    low = _iota((c, LANES), 1) < HEAD_DIM
    width = u.shape[1]
    for j in range(width // LANES):
        w0 = jnp.where(causal, ws_ref[2 * j], 0.0).astype(BF16)
        w1 = jnp.where(causal, ws_ref[2 * j + 1], 0.0).astype(BF16)
        cols = slice(j * LANES, (j + 1) * LANES)
        for ci in range(tg // c):
            rows = slice(ci * c, (ci + 1) * c)
            v2 = vn[rows, cols]
            mixed = jnp.where(low, _dot(w0, v2), _dot(w1, v2)) + bs_ref[:, cols]
            y_ref[0, rows, cols] = (u[rows, cols] * mixed).astype(BF16)


def gmlp_gating(uv, ln_g, ln_b, ws, bs, *, tg):
    b, t, w2 = uv.shape
    w = w2 // 2
    ngrp = w // HEAD_DIM
    avg = jnp.asarray(np.kron(np.eye(ngrp), np.full((HEAD_DIM, HEAD_DIM), 1.0 / HEAD_DIM)), BF16)
    bs_exp = jnp.repeat(bs.T, HEAD_DIM, axis=1)
    return pl.pallas_call(
        functools.partial(_gmlp_kernel, tg=tg),
        grid=(b, t // tg),
        in_specs=[pl.BlockSpec((1, tg, w), lambda i, j: (i, j, 0)),
                  pl.BlockSpec((1, tg, w), lambda i, j: (i, j, 1)),
                  _const_spec((1, w)), _const_spec((1, w)), _const_spec((w, w)),
                  _const_spec(ws.shape), _const_spec((GMLP_CHUNK, w))],
        out_specs=pl.BlockSpec((1, tg, w), lambda i, j: (i, j, 0)),
        out_shape=jax.ShapeDtypeStruct((b, t, w), BF16),
        compiler_params=_cp(2),
    )(uv, uv, ln_g.reshape(1, w), ln_b.reshape(1, w), avg, ws, bs_exp)


def _outproj_ffn_kernel(ya_ref, yb_ref, x_ref, wo_ref, g1_ref, n2_ref, sh_ref, sc_ref, g2_ref,
                        wup_ref, cw_ref, cb_ref, wdn_ref, o_ref, carry_ref, h_ref, up_ref, act_ref,
                        *, tm, nch, rs):
    @pl.when(pl.program_id(1) == 0)
    def _():
        carry_ref[...] = jnp.zeros(carry_ref.shape, F32)

    half = ya_ref.shape[2]
    y = _dot(ya_ref[0], wo_ref[0:half, :]) + _dot(yb_ref[0], wo_ref[half:2 * half, :])
    x1 = x_ref[0] + g1_ref[0] * y
    h_ref[...] = _norm_mod(x1, n2_ref[...], sh_ref[0], sc_ref[0]).astype(BF16)
    kconv = cw_ref.shape[1]

    def conv_rows(half, idx, r0):
        w = cw_ref[idx]
        out = cb_ref[idx]
        for d in range(kconv):
            out = out + w[kconv - 1 - d:kconv - d, :] * up_ref[half, r0 + 8 - d:r0 + 8 - d + rs, :]
        return out

    def chunk(c, acc):
        for half, idx in ((0, c), (1, nch + c)):
            up_ref[half, 0:8, :] = carry_ref[idx]
            up_ref[half, 8:tm + 8, :] = _dot(h_ref[...], wup_ref[idx])
            carry_ref[idx] = up_ref[half, tm:tm + 8, :]
        for r0 in range(0, tm, rs):
            a = conv_rows(0, c, r0)
            act_ref[r0:r0 + rs, :] = (a * _sigmoid(a) * conv_rows(1, nch + c, r0)).astype(BF16)
        return acc + _dot(act_ref[...], wdn_ref[c])

    acc = lax.fori_loop(0, nch, chunk, jnp.zeros(x1.shape, F32))
    o_ref[0] = x1 + g2_ref[0] * acc


def outproj_ffn(ya, yb, x, wo, g1, n2g, sh2, sc2, g2, wup, cw, cb, wdn, *, tm):
    b, t, d = x.shape
    half = ya.shape[2]
    nch = wdn.shape[0]
    cwid = wup.shape[2]
    row = lambda wdt: pl.BlockSpec((1, tm, wdt), lambda i, j: (i, j, 0))
    vec = pl.BlockSpec((1, 1, d), lambda i, j: (i, 0, 0))
    return pl.pallas_call(
        functools.partial(_outproj_ffn_kernel, tm=tm, nch=nch, rs=64),
        grid=(b, t // tm),
        in_specs=[row(half), row(half), row(d), _const_spec(wo.shape), vec, _const_spec((1, d)), vec, vec, vec,
                  _const_spec(wup.shape), _const_spec(cw.shape), _const_spec(cb.shape), _const_spec(wdn.shape)],
        out_specs=row(d),
        out_shape=jax.ShapeDtypeStruct((b, t, d), F32),
        scratch_shapes=[pltpu.VMEM((2 * nch, 8, cwid), F32), pltpu.VMEM((tm, d), BF16),
                        pltpu.VMEM((2, tm + 8, cwid), F32), pltpu.VMEM((tm, cwid), BF16)],
        compiler_params=_cp(2),
    )(ya, yb, x, wo, g1, n2g, sh2, sc2, g2, wup, cw, cb, wdn)


def _odd_proj_kernel(x_ref, g_ref, sh_ref, sc_ref, w_ref, mu_ref, w0_ref, wb_ref, a0_ref, ab_ref, gb_ref,
                     xi_ref, al_ref, rho_ref, ones_ref, tril_ref, tot_ref,
                     rt_ref, kt_ref, bt_ref, ktt_ref, v_ref, ktp_ref, bp_ref, pc_ref, gout_ref, bonus_ref,
                     gd_ref, xd_ref, carry_ref, *, tm, wmix, nc):
    @pl.when(pl.program_id(1) == 0)
    def _():
        carry_ref[...] = jnp.zeros(carry_ref.shape, F32)

    h = _norm_mod(x_ref[0], g_ref[...], sh_ref[0], sc_ref[0]).astype(BF16)
    pc = _dot(h, w_ref[:, 0:nc])
    pd = _dot(h, w_ref[:, nc:nc + 2 * wmix])
    gd_ref[0] = pd[:, 0:wmix]
    xd_ref[0] = pd[:, wmix:2 * wmix]

    row = _iota((tm, nc), 0)
    prev = jnp.where(row == 0, carry_ref[0:1, :], pltpu.roll(pc, 1, 0))
    carry_ref[0:1, :] = pc[tm - 1:tm, :]
    pc = pc + mu_ref[...] * (prev - pc)

    r, k, v = pc[:, 0:wmix], pc[:, wmix:2 * wmix], pc[:, 2 * wmix:3 * wmix]
    wa = pc[:, 3 * wmix:3 * wmix + LANES]
    gl = pc[:, 3 * wmix + LANES:3 * wmix + 2 * LANES]
    log_w = -math.exp(-0.5) * _sigmoid(w0_ref[...] + _dot(jnp.tanh(wa).astype(BF16), wb_ref[...]))
    a = _sigmoid(a0_ref[...] + _dot(wa.astype(BF16), ab_ref[...]))
    gout_ref[0] = _dot(_sigmoid(gl).astype(BF16), gb_ref[...])

    ones = ones_ref[...]

    def gsum(z):
        hi, lo = _split2(z)
        return _dot(hi, ones) + _dot(lo, ones)

    kap = k * xi_ref[...]
    kap = kap * lax.rsqrt(gsum(kap * kap) + EPS)
    kt = k * (1.0 + (a - 1.0) * al_ref[...])
    bonus_ref[0] = gsum(r * rho_ref[...] * kt) * v
    bvec = a * kap

    cs = _lhs_exact_dot(tril_ref[...], log_w)
    tot = _lhs_exact_dot(tot_ref[...], log_w)
    dec_out = jnp.exp(-cs)
    dec_end = jnp.exp(tot - cs)
    rt_ref[0] = (r * jnp.exp(cs)).astype(BF16)
    kt_ref[0] = (kap * jnp.exp(cs - log_w)).astype(BF16)
    bt_ref[0] = (bvec * dec_out).astype(BF16)
    ktt_ref[0] = (kt * dec_out).astype(BF16)
    v_ref[0] = v.astype(BF16)
    ktp_ref[0] = (kt * dec_end).astype(BF16)
    bp_ref[0] = (bvec * dec_end).astype(BF16)
    pc_ref[0] = jnp.exp(tot)


def odd_proj(x, g, shift, scale, w, mu, w0, wb, a0, ab, gb, xi, alpha, rho, *, tm):
    b, t, d = x.shape
    wmix = w0.shape[1]
    nc = mu.shape[1]
    nh = wmix // HEAD_DIM
    ones = jnp.asarray(np.kron(np.eye(nh), np.ones((HEAD_DIM, HEAD_DIM))), BF16)
    nchunk = tm // RWKV_CHUNK
    tril = jnp.asarray(np.kron(np.eye(nchunk), np.tril(np.ones((RWKV_CHUNK, RWKV_CHUNK)))), BF16)
    tot = jnp.asarray(np.kron(np.eye(nchunk), np.ones((RWKV_CHUNK, RWKV_CHUNK))), BF16)
    row = lambda wdt: pl.BlockSpec((1, tm, wdt), lambda i, j: (i, j, 0))
    vec = pl.BlockSpec((1, 1, d), lambda i, j: (i, 0, 0))
    cvec = _const_spec((1, wmix))
    outs = [BF16] * 7 + [F32] * 5
    return pl.pallas_call(
        functools.partial(_odd_proj_kernel, tm=tm, wmix=wmix, nc=nc),
        grid=(b, t // tm),
        in_specs=[row(d), _const_spec((1, d)), vec, vec, _const_spec(w.shape), _const_spec((1, nc)),
                  cvec, _const_spec(wb.shape), cvec, _const_spec(ab.shape), _const_spec(gb.shape),
                  cvec, cvec, cvec, _const_spec(ones.shape), _const_spec(tril.shape), _const_spec(tot.shape)],
        out_specs=[row(wmix)] * 12,
        out_shape=[jax.ShapeDtypeStruct((b, t, wmix), dt) for dt in outs],
        scratch_shapes=[pltpu.VMEM((8, nc), F32)],
        compiler_params=_cp(2),
    )(x, g, shift, scale, w, mu, w0, wb, a0, ab, gb, xi, alpha, rho, ones, tril, tot)


def _rwkv_scan_kernel(rt_ref, kt_ref, bt_ref, ktt_ref, v_ref, ktp_ref, bp_ref, pc_ref, g_ref, bonus_ref,
                      lng_ref, lnb_ref, y_ref, s_ref, *, nb, nh):
    @pl.when(pl.program_id(0) == 0)
    def _():
        s_ref[...] = jnp.zeros(s_ref.shape, F32)

    c = RWKV_CHUNK
    ti, si = _iota((2 * c, c), 0), _iota((2 * c, c), 1)
    tri = ((ti < c) & (ti > si)) | ((ti >= c) & (ti - c >= si))
    eye = jnp.where(_iota((c, c), 0) == _iota((c, c), 1), 1.0, 0.0)
    chains = [(b, h) for b in range(nb) for h in range(nh)]
    col = lambda h: slice(h * HEAD_DIM, (h + 1) * HEAD_DIM)
    ld = lambda ref: [ref[b, :, col(h)] for b, h in chains]
    kt, rt, bt, ktt, v = ld(kt_ref), ld(rt_ref), ld(bt_ref), ld(ktt_ref), ld(v_ref)
    kr = [jnp.concatenate([k_, r_], axis=0) for k_, r_ in zip(kt, rt)]
    nq = [jnp.where(tri, _dot_nt(x, y), 0.0) for x, y in zip(kr, bt)]
    aq = [jnp.where(tri, _dot_nt(x, y), 0.0).astype(BF16) for x, y in zip(kr, ktt)]
    aqv = [_dot(x, y) for x, y in zip(aq, v)]
    npow = [x[0:c] for x in nq]
    tinv = [eye - x for x in npow]
    for _ in range(int(math.log2(c)) - 1):
        npb = [x.astype(BF16) for x in npow]
        npow = [_dot(x, x) for x in npb]
        tinv = [t_ + _dot(t_.astype(BF16), p_.astype(BF16)) for t_, p_ in zip(tinv, npow)]
    s = [s_ref[i] for i in range(len(chains))]
    krs = [_dot_nt(x, s_.astype(BF16)) for x, s_ in zip(kr, s)]
    u = [_dot(t_.astype(BF16), (a_[0:c] + b_[0:c]).astype(BF16)).astype(BF16)
         for t_, a_, b_ in zip(tinv, krs, aqv)]
    y = [a_[c:2 * c] + b_[c:2 * c] - _dot(q_[c:2 * c].astype(BF16), u_)
         for a_, b_, q_, u_ in zip(krs, aqv, nq, u)]
    ktp, bp = ld(ktp_ref), ld(bp_ref)
    for i, (b, h) in enumerate(chains):
        upd = _dot_tn(jnp.concatenate([v[i], -u[i]], axis=0), jnp.concatenate([ktp[i], bp[i]], axis=0))
        s_ref[i] = s[i] * pc_ref[b, 0:1, col(h)] + upd
    for i, (b, h) in enumerate(chains):
        mu = jnp.mean(y[i], axis=1, keepdims=True)
        var = jnp.mean(jnp.square(y[i] - mu), axis=1, keepdims=True)
        yn = (y[i] - mu) * lax.rsqrt(var + EPS) * lng_ref[:, col(h)] + lnb_ref[:, col(h)]
        y_ref[b, :, col(h)] = (g_ref[b, :, col(h)] * (yn + bonus_ref[b, :, col(h)])).astype(BF16)


def rwkv_scan(rt, kt, bt, ktt, v, ktp, bp, pc, g, bonus, ln_g, ln_b):
    b, t, wmix = rt.shape
    nh = wmix // HEAD_DIM
    c = RWKV_CHUNK
    row = pl.BlockSpec((b, c, wmix), lambda j: (0, j, 0))
    cvec = _const_spec((1, wmix))
    return pl.pallas_call(
        functools.partial(_rwkv_scan_kernel, nb=b, nh=nh),
        grid=(t // c,),
        in_specs=[row] * 10 + [cvec, cvec],
        out_specs=row,
        out_shape=jax.ShapeDtypeStruct((b, t, wmix), BF16),
        scratch_shapes=[pltpu.VMEM((b * nh, HEAD_DIM, HEAD_DIM), F32)],
        compiler_params=_cp(1),
    )(rt, kt, bt, ktt, v, ktp, bp, pc, g, bonus, ln_g, ln_b)


def _lru_kernel(gd_ref, xd_ref, cw_ref, cb_ref, wa_ref, ba_ref, wx_ref, bx_ref, lam_ref, y_ref,
                xcarry_ref, hcarry_ref, *, tl, kconv):
    @pl.when(pl.program_id(1) == 0)
    def _():
        xcarry_ref[...] = jnp.zeros(xcarry_ref.shape, F32)
        hcarry_ref[...] = jnp.zeros(hcarry_ref.shape, F32)

    x = xd_ref[0]
    wdt = x.shape[1]
    row = _iota((tl, wdt), 0)
    prev = xcarry_ref[...]
    xc = cw_ref[kconv - 1:kconv, :] * x + cb_ref[...]
    for dly in range(1, kconv):
        sh = pltpu.roll(x, dly, 0)
        for r0 in range(dly):
            sh = jnp.where(row == r0, prev[8 - dly + r0:8 - dly + r0 + 1, :], sh)
        xc = xc + cw_ref[kconv - 1 - dly:kconv - dly, :] * sh
    xcarry_ref[...] = x[tl - 8:tl, :]

    xb = xc.astype(BF16)
    r = _sigmoid(_dot(xb, wa_ref[...]) + ba_ref[...])
    i = _sigmoid(_dot(xb, wx_ref[...]) + bx_ref[...])
    nl = -lam_ref[...]
    softplus = jnp.maximum(nl, 0.0) + jnp.log1p(jnp.exp(-jnp.abs(nl)))
    log_a = -LRU_C * r * softplus
    a = jnp.exp(log_a)
    th = jnp.tanh(log_a)
    bterm = jnp.sqrt(-2.0 * th / (1.0 - th)) * (i * xc)

    d = 1
    while d < tl:
        keep = row >= d
        a_sh = jnp.where(keep, pltpu.roll(a, d, 0), 1.0)
        b_sh = jnp.where(keep, pltpu.roll(bterm, d, 0), 0.0)
        bterm = a * b_sh + bterm
        a = a * a_sh
        d *= 2
    hseq = bterm + a * hcarry_ref[0:1, :]
    hcarry_ref[0:1, :] = hseq[tl - 1:tl, :]
    y_ref[0] = (_gelu(gd_ref[0]) * hseq).astype(BF16)


def rglru(gd, xd, conv_w, conv_b, wa_bd, ba, wx_bd, bx, lam, *, tl):
    b, t, wdt = xd.shape
    kconv = conv_w.shape[0]
    row = pl.BlockSpec((1, tl, wdt), lambda i, j: (i, j, 0))
    cvec = _const_spec((1, wdt))
    return pl.pallas_call(
        functools.partial(_lru_kernel, tl=tl, kconv=kconv),
        grid=(b, t // tl),
        in_specs=[row, row, _const_spec(conv_w.shape), cvec, _const_spec(wa_bd.shape), cvec,
                  _const_spec(wx_bd.shape), cvec, cvec],
        out_specs=row,
        out_shape=jax.ShapeDtypeStruct((b, t, wdt), BF16),
        scratch_shapes=[pltpu.VMEM((8, wdt), F32), pltpu.VMEM((8, wdt), F32)],
        compiler_params=_cp(2),
    )(gd, xd, conv_w, conv_b, wa_bd, ba, wx_bd, bx, lam)


def _final_norm_kernel(x_ref, g_ref, o_ref):
    x = x_ref[0]
    o_ref[0] = x * lax.rsqrt(jnp.mean(x * x, axis=-1, keepdims=True) + EPS) * g_ref[...]


def final_norm(x, g, *, tm):
    b, t, d = x.shape
    return pl.pallas_call(
        _final_norm_kernel,
        grid=(b, t // tm),
        in_specs=[pl.BlockSpec((1, tm, d), lambda i, j: (i, j, 0)), _const_spec((1, d))],
        out_specs=pl.BlockSpec((1, tm, d), lambda i, j: (i, j, 0)),
        out_shape=jax.ShapeDtypeStruct((b, t, d), F32),
        compiler_params=_cp(2),
    )(x, g.reshape(1, d))


def _block_diag(w):
    n, a, b = w.shape
    eye = jnp.eye(n, dtype=w.dtype)
    return (eye[:, None, :, None] * w[:, :, None, :]).reshape(n * a, n * b)


def _even_weights(w_in, mix_half):
    d = w_in.shape[0]
    nh = mix_half // HEAD_DIM
    hk = nh // NSA_GROUP
    nkv = hk * HEAD_DIM
    o = 0
    wq = w_in[:, o:o + mix_half].reshape(d, nh, HEAD_DIM); o += mix_half
    kvs = []
    for _ in range(6):
        kvs.append(w_in[:, o:o + nkv]); o += nkv
    wg = w_in[:, o:o + 3 * nh].reshape(d, hk, 3 * NSA_GROUP); o += 3 * nh
    wuv = w_in[:, o:]
    kv_of = jnp.arange(nh) // NSA_GROUP
    place = jax.nn.one_hot(kv_of, LANES // HEAD_DIM, dtype=w_in.dtype)
    wq_pad = (wq[:, :, None, :] * place[None, :, :, None]).reshape(d, nh * LANES)
    wg_pad = jnp.pad(wg, ((0, 0), (0, 0), (0, LANES - 3 * NSA_GROUP))).reshape(d, hk * LANES)
    return jnp.concatenate([wq_pad] + kvs + [wg_pad, wuv], axis=1).astype(BF16)


def _compress_weights(pe, w1, w2, hk):
    hid = w1.shape[1]
    pe2 = jnp.tile(pe, (1, hk))
    w1r = w1.reshape(CMP_BLOCK, HEAD_DIM, hid)
    eye = jnp.eye(hk, dtype=w1.dtype)
    w1_bd = (eye[None, :, None, :, None] * w1r[:, None, :, None, :]).reshape(CMP_BLOCK, hk * HEAD_DIM, hk * hid)
    w2_bd = (eye[:, None, :, None] * w2[None, :, None, :]).reshape(hk * hid, hk * HEAD_DIM)
    return pe2, w1_bd.astype(BF16), w2_bd.astype(BF16)


def _ffn_weights(w_up, conv_w, conv_b, w_down, cwid):
    d, f2 = w_up.shape
    n2 = f2 // cwid
    wup = w_up.reshape(d, n2, cwid).transpose(1, 0, 2).astype(BF16)
    cw = conv_w.reshape(conv_w.shape[0], n2, cwid).transpose(1, 0, 2)
    cb = conv_b.reshape(n2, 1, cwid)
    wdn = w_down.reshape(n2 // 2, cwid, w_down.shape[1]).astype(BF16)
    return wup, cw, cb, wdn


def kernel(x, c, positions, ada_w, ada_b, norm1_g, norm2_g, ev_w_in, ev_w_out, nsa_pe_k, nsa_w1_k, nsa_w2_k, nsa_pe_v, nsa_w1_v, nsa_w2_v, gmlp_ln_g, gmlp_ln_b, gmlp_ws, gmlp_bs, od_w_in, od_w_out, rwkv_mu, rwkv_w0, rwkv_wB, rwkv_a0, rwkv_aB, rwkv_gB, rwkv_xi, rwkv_alpha, rwkv_rho, rwkv_ln_g, rwkv_ln_b, lru_conv_w, lru_conv_b, lru_wa, lru_ba, lru_wx, lru_bx, lru_lambda, ffn_up, ffn_conv_w, ffn_conv_b, ffn_down, final_g):
    b, t, d = x.shape
    depth = ada_w.shape[0]
    mix_half = d // 2
    hk = mix_half // HEAD_DIM // NSA_GROUP
    assert hk * HEAD_DIM == LANES and t % 512 == 0
    tm = 512

    mod = adaln_mod(c, ada_w, ada_b)
    mods = mod.reshape(depth, b, 6, 1, d)
    tabs = rope_tables(positions, tm)
    n_cmp = (t - CMP_BLOCK) // CMP_STRIDE + 1
    ngrp = t // CMP_STRIDE
    pos_end = jnp.pad(positions[:, CMP_BLOCK - 1::CMP_STRIDE][:, :n_cmp], ((0, 0), (0, ngrp - n_cmp)))
    end_tabs = rope_tables(pos_end, ngrp)

    for layer in range(depth):
        i = layer // 2
        sh1, sc1, g1, sh2, sc2, g2 = (mods[layer, :, j] for j in range(6))
        n1 = norm1_g[layer].reshape(1, d)
        n2 = norm2_g[layer].reshape(1, d)
        if layer % 2 == 0:
            w_in = _even_weights(ev_w_in[i], mix_half)
            q, kc, vc, ks, vst, kw, vwt, gates, uv = even_proj(x, n1, sh1, sc1, w_in, tabs, tm=tm, tk_sel=512,
                                                             tk_win=128)
            pek, w1k, w2k = _compress_weights(nsa_pe_k[i], nsa_w1_k[i], nsa_w2_k[i], hk)
            pev, w1v, w2v = _compress_weights(nsa_pe_v[i], nsa_w1_v[i], nsa_w2_v[i], hk)
            kcmp, vcmp = nsa_compress(kc, vc, pek, w1k, w2k, pev, w1v, w2v, end_tabs)
            oc, sel_t = nsa_cmp_topk(q, kcmp, vcmp, tq=256)
            ya = nsa_sel_win(q, ks, vst, kw, vwt, sel_t, oc, gates, tq=128)
            yb = gmlp_gating(uv, gmlp_ln_g[i], gmlp_ln_b[i], gmlp_ws[i], gmlp_bs[i], tg=tm)
            wo = ev_w_out[i].astype(BF16)
        else:
            w_in = od_w_in[i].astype(BF16)
            lora = rwkv_wB.shape[1]
            wb = jnp.pad(rwkv_wB[i], ((0, LANES - lora), (0, 0))).astype(BF16)
            ab = jnp.pad(rwkv_aB[i], ((LANES - rwkv_aB.shape[1], 0), (0, 0))).astype(BF16)
            v1 = lambda p: p.reshape(1, -1)
            (rt, kt, bt, ktt, v, ktp, bp, pc, gout, bonus, gd, xd) = odd_proj(
                x, n1, sh1, sc1, w_in, v1(rwkv_mu[i]), v1(rwkv_w0[i]), wb, v1(rwkv_a0[i]), ab,
                rwkv_gB[i].astype(BF16), v1(rwkv_xi[i]), v1(rwkv_alpha[i]), v1(rwkv_rho[i]), tm=tm)
            ya = rwkv_scan(rt, kt, bt, ktt, v, ktp, bp, pc, gout, bonus, v1(rwkv_ln_g[i]), v1(rwkv_ln_b[i]))
            yb = rglru(gd, xd, lru_conv_w[i], v1(lru_conv_b[i]), _block_diag(lru_wa[i]).astype(BF16),
                       v1(lru_ba[i]), _block_diag(lru_wx[i]).astype(BF16), v1(lru_bx[i]), v1(lru_lambda[i]), tl=tm)
            wo = od_w_out[i].astype(BF16)
        wup, cw, cb, wdn = _ffn_weights(ffn_up[layer], ffn_conv_w[layer], ffn_conv_b[layer], ffn_down[layer], 256)
        x = outproj_ffn(ya, yb, x, wo, g1, n2, sh2, sc2, g2, wup, cw, cb, wdn, tm=tm)
    return final_norm(x, final_g, tm=tm)
```

```python
import functools
import math

import numpy as np
import jax
import jax.numpy as jnp
from jax import lax
from jax.experimental import pallas as pl
from jax.experimental.pallas import tpu as pltpu

F32 = jnp.float32
BF16 = jnp.bfloat16

HEAD_DIM = 64
ROPE_DIM = HEAD_DIM // 4
ROPE_THETA = 500000.0
EPS = 1e-6
NEG_INF = -1e30
LANES = 128

NSA_GROUP = 4
CMP_BLOCK = 32
CMP_STRIDE = 16
SLC_BLOCK = 64
SLC_SHIFT = 6
SLC_TOPK = 16
WINDOW = 512
FORCE_SCORE = 1e3
MASK_BIG = -NEG_INF
LOG2E = math.log2(math.e)
GMLP_CHUNK = 128
RWKV_CHUNK = 64
LRU_C = 8.0

VMEM_LIMIT = 56 * 1024 * 1024


def _cp(n_axes, vmem=VMEM_LIMIT):
    return pltpu.CompilerParams(dimension_semantics=("arbitrary",) * n_axes, vmem_limit_bytes=vmem)


def _const_spec(shape):
    nd = len(shape)
    return pl.BlockSpec(shape, lambda *_: (0,) * nd, pipeline_mode=pl.Buffered(1))


def _dot(a, b):
    return jnp.dot(a, b, preferred_element_type=F32)


def _dot_nt(a, b):
    return lax.dot_general(a, b, (((1,), (1,)), ((), ())), preferred_element_type=F32)


def _dot_tn(a, b):
    return lax.dot_general(a, b, (((0,), (0,)), ((), ())), preferred_element_type=F32)


def _split2(z):
    hi = z.astype(BF16)
    lo = (z - hi.astype(F32)).astype(BF16)
    return hi, lo


def _split3(z):
    hi = z.astype(BF16)
    r = z - hi.astype(F32)
    mid = r.astype(BF16)
    lo = (r - mid.astype(F32)).astype(BF16)
    return hi, mid, lo


def _dot_exact_rhs(z, m):
    hi, mid, lo = _split3(z)
    return _dot(hi, m) + _dot(mid, m) + _dot(lo, m)


def _lhs_exact_dot(m, z):
    hi, mid, lo = _split3(z)
    return _dot(m, hi) + _dot(m, mid) + _dot(m, lo)


def _norm_mod(x, g, shift, scale):
    y = x * lax.rsqrt(jnp.mean(x * x, axis=-1, keepdims=True) + EPS)
    return (y * g) * (1.0 + scale) + shift


def _gelu(x):
    return jax.nn.gelu(x, approximate=True)


def _sigmoid(x):
    return jax.nn.sigmoid(x)


def _iota(shape, axis):
    return lax.broadcasted_iota(jnp.int32, shape, axis)


def _adaln_kernel(c_ref, w_ref, b_ref, o_ref):
    cond = c_ref[...]
    cond = cond * _sigmoid(cond)
    o_ref[0] = _dot(cond.astype(BF16), w_ref[0].astype(BF16)) + b_ref[0]


def adaln_mod(c, ada_w, ada_b):
    depth, d, n = ada_w.shape
    b = c.shape[0]
    rows = 16
    cp = jnp.zeros((rows, d), F32).at[:b].set(c)
    tn = 1536
    out = pl.pallas_call(
        _adaln_kernel,
        grid=(depth, n // tn),
        in_specs=[pl.BlockSpec((rows, d), lambda l, j: (0, 0)),
                  pl.BlockSpec((1, d, tn), lambda l, j: (l, 0, j)),
                  pl.BlockSpec((1, 1, tn), lambda l, j: (l, 0, j))],
        out_specs=pl.BlockSpec((1, rows, tn), lambda l, j: (l, 0, j)),
        out_shape=jax.ShapeDtypeStruct((depth, rows, n), F32),
        compiler_params=_cp(2),
    )(cp, ada_w, ada_b.reshape(depth, 1, n))
    return out[:, :b]


def _rope_kernel(pos_ref, inv_ref, c_ref, sa_ref, sb_ref):
    ang = pos_ref[0].astype(F32) * inv_ref[...]
    cos, sin = jnp.cos(ang), jnp.sin(ang)
    d = _iota(ang.shape, 1) % HEAD_DIM
    c_ref[0] = jnp.where(d < ROPE_DIM, cos, 1.0)
    sa_ref[0] = jnp.where(d < ROPE_DIM // 2, -sin, 0.0)
    sb_ref[0] = jnp.where((d >= ROPE_DIM // 2) & (d < ROPE_DIM), sin, 0.0)


def rope_tables(pos, tm):
    b, n = pos.shape
    half = ROPE_DIM // 2
    inv = ROPE_THETA ** (-jnp.arange(0, ROPE_DIM, 2, dtype=F32) / ROPE_DIM)
    inv_lane = jnp.tile(inv, LANES // half).reshape(1, LANES)
    spec = pl.BlockSpec((1, tm, LANES), lambda i, j: (i, j, 0))
    return pl.pallas_call(
        _rope_kernel,
        grid=(b, n // tm),
        in_specs=[pl.BlockSpec((1, tm, 1), lambda i, j: (i, j, 0)),
                  pl.BlockSpec((1, LANES), lambda i, j: (0, 0))],
        out_specs=[spec, spec, spec],
        out_shape=[jax.ShapeDtypeStruct((b, n, LANES), F32)] * 3,
        compiler_params=_cp(2),
    )(pos[..., None], inv_lane)


def _rope(y, c, sa, sb):
    return y * c + pltpu.roll(y, LANES - ROPE_DIM // 2, 1) * sa + pltpu.roll(y, ROPE_DIM // 2, 1) * sb


def _even_proj_kernel(x_ref, g_ref, sh_ref, sc_ref, w_ref, c_ref, sa_ref, sb_ref,
                      q_ref, kc_ref, vc_ref, ks_ref, vst_ref, kw_ref, vwt_ref, gate_ref, uv_ref,
                      *, nq, nuv):
    h = _norm_mod(x_ref[0], g_ref[...], sh_ref[0], sc_ref[0]).astype(BF16)
    c, sa, sb = c_ref[0], sa_ref[0], sb_ref[0]
    scale = HEAD_DIM ** -0.5 * LOG2E
    yq = _dot(h, w_ref[:, 0:nq])
    for j in range(nq // LANES):
        sl = slice(j * LANES, (j + 1) * LANES)
        q_ref[0, :, sl] = (_rope(yq[:, sl], c, sa, sb) * scale).astype(BF16)
    o = nq
    ykv = _dot(h, w_ref[:, o:o + 6 * LANES])
    kc_ref[0] = ykv[:, 0:LANES]
    vc_ref[0] = ykv[:, LANES:2 * LANES]
    tm = ykv.shape[0]
    blk = jnp.right_shift(pl.program_id(1) * tm + _iota((tm, LANES), 0), SLC_SHIFT)
    ks_ref[0, :, 0:LANES] = _rope(ykv[:, 2 * LANES:3 * LANES], c, sa, sb).astype(BF16)
    ks_ref[0, :, LANES:2 * LANES] = jnp.where(_iota((tm, LANES), 1) == blk, MASK_BIG, 0.0).astype(BF16)
    kw_ref[0] = _rope(ykv[:, 4 * LANES:5 * LANES], c, sa, sb).astype(BF16)
    for ref, lo in ((vst_ref, 3 * LANES), (vwt_ref, 5 * LANES)):
        tile = ref.shape[3]
        for i in range(ref.shape[1]):
            ref[0, i] = ykv[i * tile:(i + 1) * tile, lo:lo + LANES].T.astype(BF16)
    o += 6 * LANES
    gate_ref[0] = _sigmoid(_dot(h, w_ref[:, o:o + 2 * LANES]))
    o += 2 * LANES
    uv_ref[0] = _dot(h, w_ref[:, o:o + nuv])


def even_proj(x, g, shift, scale, w, tabs, *, tm, tk_sel, tk_win):
    b, t, d = x.shape
    n = w.shape[1]
    nq = 2 * NSA_GROUP * LANES
    nuv = n - nq - 8 * LANES
    row = lambda wdt: pl.BlockSpec((1, tm, wdt), lambda i, j: (i, j, 0))
    vec = pl.BlockSpec((1, 1, d), lambda i, j: (i, 0, 0))
    rows = lambda wdt, dt: (row(wdt), jax.ShapeDtypeStruct((b, t, wdt), dt))
    vt = lambda tk: (pl.BlockSpec((1, tm // tk, LANES, tk), lambda i, j: (i, j, 0, 0)),
                     jax.ShapeDtypeStruct((b, t // tk, LANES, tk), BF16))
    outs = [rows(nq, BF16), rows(LANES, F32), rows(LANES, F32), rows(2 * LANES, BF16), vt(tk_sel),
            rows(LANES, BF16), vt(tk_win), rows(2 * LANES, F32), rows(nuv, F32)]
    return pl.pallas_call(
        functools.partial(_even_proj_kernel, nq=nq, nuv=nuv),
        grid=(b, t // tm),
        in_specs=[row(d), _const_spec((1, d)), vec, vec, _const_spec((d, n)),
                  row(LANES), row(LANES), row(LANES)],
        out_specs=[s for s, _ in outs],
        out_shape=[s for _, s in outs],
        compiler_params=_cp(2),
    )(x, g, shift, scale, w, *tabs)


def _compress_kernel(kc_ref, vc_ref, pek_ref, w1k_ref, w2k_ref, pev_ref, w1v_ref, w2v_ref,
                     c_ref, sa_ref, sb_ref, ko_ref, vo_ref, *, ngrp, n_cmp):
    half = CMP_BLOCK // 2

    def phi(src_ref, pe_ref, w1_ref, w2_ref):
        p = jnp.zeros((ngrp, w1_ref.shape[2]), F32)
        q = jnp.zeros((ngrp, w1_ref.shape[2]), F32)
        for l in range(half):
            a = src_ref[0, pl.ds(l, ngrp, stride=CMP_STRIDE), :]
            p = p + _dot((a + pe_ref[l:l + 1, :]).astype(BF16), w1_ref[l])
            q = q + _dot((a + pe_ref[half + l:half + l + 1, :]).astype(BF16), w1_ref[half + l])
        hid = p + pltpu.roll(q, ngrp - 1, 0)
        return _dot(_gelu(hid).astype(BF16), w2_ref[...])

    row = _iota((ngrp, LANES), 0)
    kc = _rope(phi(kc_ref, pek_ref, w1k_ref, w2k_ref), c_ref[0], sa_ref[0], sb_ref[0])
    ko_ref[0] = jnp.where(row < n_cmp, kc, 0.0).astype(BF16)
    vc = phi(vc_ref, pev_ref, w1v_ref, w2v_ref)
    vo_ref[0] = jnp.where(row < n_cmp, vc, 0.0).astype(BF16)


def nsa_compress(kc, vc, pek, w1k, w2k, pev, w1v, w2v, end_tabs):
    b, t, _ = kc.shape
    ngrp = t // CMP_STRIDE
    n_cmp = (t - CMP_BLOCK) // CMP_STRIDE + 1
    tok = pl.BlockSpec((1, t, LANES), lambda i: (i, 0, 0))
    grp = pl.BlockSpec((1, ngrp, LANES), lambda i: (i, 0, 0))
    return pl.pallas_call(
        functools.partial(_compress_kernel, ngrp=ngrp, n_cmp=n_cmp),
        grid=(b,),
        in_specs=[tok, tok, _const_spec(pek.shape), _const_spec(w1k.shape), _const_spec(w2k.shape),
                  _const_spec(pev.shape), _const_spec(w1v.shape), _const_spec(w2v.shape), grp, grp, grp],
        out_specs=[grp, grp],
        out_shape=[jax.ShapeDtypeStruct((b, ngrp, LANES), BF16)] * 2,
        compiler_params=_cp(1),
    )(kc, vc, pek, w1k, w2k, pev, w1v, w2v, *end_tabs)


def _cmp_topk_kernel(q_ref, kc_ref, vc_ref, ovt_ref, oc_ref, sel_ref, *, tq, n_cmp, ncp, n_slc, k_top):
    t0 = pl.program_id(2) * tq
    kc, vc = kc_ref[0], vc_ref[0]
    end = CMP_BLOCK - 1
    n_col, t_row = _iota((tq, ncp), 1), t0 + _iota((tq, ncp), 0)
    valid = (n_col * CMP_STRIDE + end <= t_row) & (n_col < n_cmp)
    n_row, t_col = _iota((ncp, tq), 0), t0 + _iota((ncp, tq), 1)
    valid_t = (n_row * CMP_STRIDE + end <= t_col) & (n_row < n_cmp)
    psum_t = jnp.zeros((ncp, tq), F32)
    for g in range(NSA_GROUP):
        qg = q_ref[0, :, g * LANES:(g + 1) * LANES]
        s = jnp.where(valid, _dot_nt(qg, kc), NEG_INF)
        e = jnp.where(valid, jnp.exp2(s - jnp.max(s, axis=1, keepdims=True)), 0.0)
        l = jnp.sum(e, axis=1, keepdims=True)
        p = e / jnp.where(l > 0.0, l, 1.0)
        oc_ref[0, :, g * LANES:(g + 1) * LANES] = _dot(p.astype(BF16), vc)
        st = jnp.where(valid_t, _dot_nt(kc, qg), NEG_INF)
        et = jnp.where(valid_t, jnp.exp2(st - jnp.max(st, axis=0, keepdims=True)), 0.0)
        lt = jnp.sum(et, axis=0, keepdims=True)
        psum_t = psum_t + et / jnp.where(lt > 0.0, lt, 1.0)
    hi, lo = _split2(psum_t)
    imp = _dot(ovt_ref[...], hi) + _dot(ovt_ref[...], lo)
    j = _iota((n_slc, tq), 0)
    t = t0 + _iota((n_slc, tq), 1)
    cur = jnp.right_shift(t, SLC_SHIFT)
    forced = (j == 0) | (j == cur) | (j == cur - 1)
    score = jnp.where(forced, FORCE_SCORE, jnp.where(j * SLC_BLOCK <= t, imp, -1.0))
    rank = jnp.zeros((n_slc, tq), jnp.int32)
    for jj in range(n_slc):
        r = score[jj:jj + 1, :]
        beats = (r > score) | ((r == score) & (j > jj))
        rank = rank + beats.astype(jnp.int32)
    sel_ref[0, 0] = jnp.where(rank < k_top, 1.0, 0.0).astype(BF16)


def nsa_cmp_topk(q, kcmp, vcmp, *, tq):
    b, t, nq = q.shape
    hk = nq // (NSA_GROUP * LANES)
    ncp = kcmp.shape[1]
    n_cmp = (t - CMP_BLOCK) // CMP_STRIDE + 1
    n_slc = t // SLC_BLOCK
    cs = np.arange(ncp) * CMP_STRIDE
    ss = np.arange(n_slc) * SLC_BLOCK
    ov = np.clip(np.minimum(cs[:, None] + CMP_BLOCK, ss[None, :] + SLC_BLOCK)
                 - np.maximum(cs[:, None], ss[None, :]), 0, None).astype(np.float32) / CMP_BLOCK
    ov[n_cmp:] = 0.0
    ovt = jnp.asarray(ov.T, BF16)
    gq = NSA_GROUP * LANES
    return pl.pallas_call(
        functools.partial(_cmp_topk_kernel, tq=tq, n_cmp=n_cmp, ncp=ncp, n_slc=n_slc,
                          k_top=min(SLC_TOPK, n_slc)),
        grid=(b, hk, t // tq),
        in_specs=[pl.BlockSpec((1, tq, gq), lambda i, k, j: (i, j, k)),
                  pl.BlockSpec((1, ncp, LANES), lambda i, k, j: (i, 0, 0)),
                  pl.BlockSpec((1, ncp, LANES), lambda i, k, j: (i, 0, 0)),
                  _const_spec((n_slc, ncp))],
        out_specs=[pl.BlockSpec((1, tq, gq), lambda i, k, j: (i, j, k)),
                   pl.BlockSpec((1, 1, n_slc, tq), lambda i, k, j: (i, k, 0, j))],
        out_shape=[jax.ShapeDtypeStruct((b, t, nq), F32),
                   jax.ShapeDtypeStruct((b, hk, n_slc, t), BF16)],
        compiler_params=_cp(3),
    )(q, kcmp, vcmp, ovt)


def _sel_win_kernel(q_ref, ks_ref, vst_ref, kw_ref, vwt_ref, sel_ref, oc_ref, gate_ref, y_ref,
                    *, tq, tk, n_slc):
    kv = pl.program_id(1)
    qi = pl.program_id(2)
    q0 = qi * tq
    grp = NSA_GROUP
    heads = lambda z: jnp.concatenate([z] * grp, axis=1)
    qt = jnp.concatenate([q_ref[0, :, g * LANES:(g + 1) * LANES].astype(F32).T.astype(BF16)
                          for g in range(grp)], axis=1)
    unsel = (sel_ref[0, 0].astype(F32) - 1.0).astype(BF16)
    unsel = jnp.concatenate([unsel, jnp.zeros((LANES - n_slc, tq), BF16)], axis=0)
    qa = jnp.concatenate([qt, heads(unsel)], axis=0)

    def scores(kt):
        k0 = pl.multiple_of(kt * tk, tk)
        return _dot(ks_ref[0, pl.ds(k0, tk), :], qa)

    def update(s, kt, m, l, acc):
        m_new = jnp.maximum(m, jnp.max(s, axis=0, keepdims=True))
        alpha = jnp.exp2(m - m_new)
        p = jnp.exp2(s - m_new)
        l = alpha * l + jnp.sum(p, axis=0, keepdims=True)
        acc = alpha * acc + _dot(vst_ref[0, kt], p.astype(BF16))
        return m_new, l, acc

    kt_diag = q0 // tk
    causal = kt_diag * tk + _iota((tk, tq), 0) <= q0 + _iota((tk, tq), 1)
    s_first = scores(kt_diag) + heads(jnp.where(causal, 0.0, NEG_INF))

    def sel_body(i, carry):
        s_cur, m, l, acc = carry
        s_next = scores(i)
        m, l, acc = update(s_cur, jnp.where(i == 0, kt_diag, i - 1), m, l, acc)
        return s_next, m, l, acc

    init = (s_first, jnp.full((1, grp * tq), NEG_INF, F32), jnp.zeros((1, grp * tq), F32),
            jnp.zeros((LANES, grp * tq), F32))
    s_last, m_s, l_s, acc_s = lax.fori_loop(0, kt_diag, sel_body, init)
    _, l_s, acc_s = update(s_last, jnp.maximum(kt_diag - 1, 0), m_s, l_s, acc_s)
    o_s4 = acc_s / l_s
    o_s = [o_s4[:, g * tq:(g + 1) * tq].T for g in range(grp)]

    nwt = WINDOW // tq + 1
    kt0 = jnp.maximum(qi - WINDOW // tq, 0)
    k0 = pl.multiple_of(kt0 * tq, tq)
    kp, tt = k0 + _iota((nwt * tq, tq), 0), q0 + _iota((nwt * tq, tq), 1)
    bias_w = jnp.where((kp <= tt) & (kp > tt - WINDOW), 0.0, NEG_INF)
    sw = _dot(kw_ref[0, pl.ds(k0, nwt * tq), :], qt) + heads(bias_w)
    pw = jnp.exp2(sw - jnp.max(sw, axis=0, keepdims=True))
    l_w = jnp.sum(pw, axis=0, keepdims=True)
    pb = pw.astype(BF16)
    acc_w = _dot(vwt_ref[0, kt0], pb[0:tq])
    for i in range(1, nwt):
        acc_w = acc_w + _dot(vwt_ref[0, kt0 + i], pb[i * tq:(i + 1) * tq])
    o_w4 = acc_w / l_w
    o_w = [o_w4[:, g * tq:(g + 1) * tq].T for g in range(grp)]

    gate = gate_ref[0]
    ys = []
    for g in range(NSA_GROUP):
        y = (gate[:, 3 * g:3 * g + 1] * oc_ref[0, :, g * LANES:(g + 1) * LANES]
             + gate[:, 3 * g + 1:3 * g + 2] * o_s[g] + gate[:, 3 * g + 2:3 * g + 3] * o_w[g])
        ys.append(y)
    low = _iota((tq, LANES), 1) < HEAD_DIM
    first_kv = kv == 0
    for j in range(NSA_GROUP // 2):
        a, b2 = ys[2 * j], ys[2 * j + 1]
        lo_half = jnp.where(first_kv, a, pltpu.roll(a, HEAD_DIM, 1))
        hi_half = jnp.where(first_kv, pltpu.roll(b2, HEAD_DIM, 1), b2)
        y_ref[0, :, j * LANES:(j + 1) * LANES] = jnp.where(low, lo_half, hi_half).astype(BF16)


def nsa_sel_win(q, ks, vst, kw, vwt, sel_t, oc, gates, *, tq):
    b, t, nq = q.shape
    hk = nq // (NSA_GROUP * LANES)
    n_slc = t // SLC_BLOCK
    tk = vst.shape[3]
    assert vwt.shape[3] == tq and tk % tq == 0
    gq = NSA_GROUP * LANES
    full = pl.BlockSpec((1, t, LANES), lambda i, k, j: (i, 0, 0))
    tiles = lambda a: pl.BlockSpec((1,) + a.shape[1:], lambda i, k, j: (i, 0, 0, 0))
    qspec = pl.BlockSpec((1, tq, gq), lambda i, k, j: (i, j, k))
    return pl.pallas_call(
        functools.partial(_sel_win_kernel, tq=tq, tk=tk, n_slc=n_slc),
        grid=(b, hk, t // tq),
        in_specs=[qspec, pl.BlockSpec((1, t, 2 * LANES), lambda i, k, j: (i, 0, 0)), tiles(vst), full, tiles(vwt),
                  pl.BlockSpec((1, 1, n_slc, tq), lambda i, k, j: (i, k, 0, j)),
                  qspec,
                  pl.BlockSpec((1, tq, LANES), lambda i, k, j: (i, j, k))],
        out_specs=pl.BlockSpec((1, tq, NSA_GROUP * HEAD_DIM), lambda i, k, j: (i, j, k)),
        out_shape=jax.ShapeDtypeStruct((b, t, hk * NSA_GROUP * HEAD_DIM), BF16),
        compiler_params=_cp(3),
    )(q, ks, vst, kw, vwt, sel_t, oc, gates)


def _gmlp_kernel(u_ref, v_ref, g_ref, b_ref, avg_ref, ws_ref, bs_ref, y_ref, *, tg):
    u = _gelu(u_ref[0])
    v = _gelu(v_ref[0])
    avg = avg_ref[...]

    def gmean(z):
        hi, lo = _split2(z)
        return _dot(hi, avg) + _dot(lo, avg)

    d = v - gmean(v)
    vn = (d * lax.rsqrt(gmean(d * d) + EPS) * g_ref[...] + b_ref[...]).astype(BF16)
    c = GMLP_CHUNK
    causal = _iota((c, c), 0) >= _iota((c, c), 1)
    low = _iota((c, LANES), 1) < HEAD_DIM
    width = u.shape[1]
    for j in range(width // LANES):
        w0 = jnp.where(causal, ws_ref[2 * j], 0.0).astype(BF16)
        w1 = jnp.where(causal, ws_ref[2 * j + 1], 0.0).astype(BF16)
        cols = slice(j * LANES, (j + 1) * LANES)
        for ci in range(tg // c):
            rows = slice(ci * c, (ci + 1) * c)
            v2 = vn[rows, cols]
            mixed = jnp.where(low, _dot(w0, v2), _dot(w1, v2)) + bs_ref[:, cols]
            y_ref[0, rows, cols] = (u[rows, cols] * mixed).astype(BF16)


def gmlp_gating(uv, ln_g, ln_b, ws, bs, *, tg):
    b, t, w2 = uv.shape
    w = w2 // 2
    ngrp = w // HEAD_DIM
    avg = jnp.asarray(np.kron(np.eye(ngrp), np.full((HEAD_DIM, HEAD_DIM), 1.0 / HEAD_DIM)), BF16)
    bs_exp = jnp.repeat(bs.T, HEAD_DIM, axis=1)
    return pl.pallas_call(
        functools.partial(_gmlp_kernel, tg=tg),
        grid=(b, t // tg),
        in_specs=[pl.BlockSpec((1, tg, w), lambda i, j: (i, j, 0)),
                  pl.BlockSpec((1, tg, w), lambda i, j: (i, j, 1)),
                  _const_spec((1, w)), _const_spec((1, w)), _const_spec((w, w)),
                  _const_spec(ws.shape), _const_spec((GMLP_CHUNK, w))],
        out_specs=pl.BlockSpec((1, tg, w), lambda i, j: (i, j, 0)),
        out_shape=jax.ShapeDtypeStruct((b, t, w), BF16),
        compiler_params=_cp(2),
    )(uv, uv, ln_g.reshape(1, w), ln_b.reshape(1, w), avg, ws, bs_exp)


def _outproj_ffn_kernel(ya_ref, yb_ref, x_ref, wo_ref, g1_ref, n2_ref, sh_ref, sc_ref, g2_ref,
                        wup_ref, cw_ref, cb_ref, wdn_ref, o_ref, carry_ref, h_ref, up_ref, act_ref,
                        *, tm, nch, cwid, rs):
    @pl.when(pl.program_id(1) == 0)
    def _():
        carry_ref[...] = jnp.zeros(carry_ref.shape, F32)

    half = ya_ref.shape[2]
    y = _dot(ya_ref[0], wo_ref[0:half, :]) + _dot(yb_ref[0], wo_ref[half:2 * half, :])
    x1 = x_ref[0] + g1_ref[0] * y
    o_ref[0] = x1
    h_ref[...] = _norm_mod(x1, n2_ref[...], sh_ref[0], sc_ref[0]).astype(BF16)
    kconv = cw_ref.shape[0]
    cols = lambda idx: slice(idx * cwid, (idx + 1) * cwid)

    def conv_rows(slot, half, idx, r0):
        w = cw_ref[:, cols(idx)]
        out = cb_ref[:, cols(idx)]
        for d in range(kconv):
            out = out + w[kconv - 1 - d:kconv - d, :] * up_ref[slot, half, r0 + 8 - d:r0 + 8 - d + rs, :]
        return out

    for c in range(nch):
        slot = c % 2
        for half, idx in ((0, c), (1, nch + c)):
            up_ref[slot, half, 0:8, :] = carry_ref[idx]
            up_ref[slot, half, 8:tm + 8, :] = _dot(h_ref[...], wup_ref[:, cols(idx)])
            carry_ref[idx] = up_ref[slot, half, tm:tm + 8, :]
        for r0 in range(0, tm, rs):
            a = conv_rows(slot, 0, c, r0)
            act = a * _sigmoid(a) * conv_rows(slot, 1, nch + c, r0)
            act_ref[r0:r0 + rs, cols(c)] = act.astype(BF16)
    o_ref[0] = o_ref[0] + g2_ref[0] * _dot(act_ref[...], wdn_ref[...])


def outproj_ffn(ya, yb, x, wo, g1, n2g, sh2, sc2, g2, wup, cw, cb, wdn, *, tm, cwid):
    b, t, d = x.shape
    half = ya.shape[2]
    nch = wdn.shape[0] // cwid
    assert nch * cwid == wdn.shape[0] and wup.shape[1] == 2 * nch * cwid
    row = lambda wdt: pl.BlockSpec((1, tm, wdt), lambda i, j: (i, j, 0))
    vec = pl.BlockSpec((1, 1, d), lambda i, j: (i, 0, 0))
    return pl.pallas_call(
        functools.partial(_outproj_ffn_kernel, tm=tm, nch=nch, cwid=cwid, rs=64),
        grid=(b, t // tm),
        in_specs=[row(half), row(half), row(d), _const_spec(wo.shape), vec, _const_spec((1, d)), vec, vec, vec,
                  _const_spec(wup.shape), _const_spec(cw.shape), _const_spec(cb.shape), _const_spec(wdn.shape)],
        out_specs=row(d),
        out_shape=jax.ShapeDtypeStruct((b, t, d), F32),
        scratch_shapes=[pltpu.VMEM((2 * nch, 8, cwid), F32), pltpu.VMEM((tm, d), BF16),
                        pltpu.VMEM((2, 2, tm + 8, cwid), F32), pltpu.VMEM((tm, nch * cwid), BF16)],
        compiler_params=_cp(2),
    )(ya, yb, x, wo, g1, n2g, sh2, sc2, g2, wup, cw, cb, wdn)


def _odd_proj_kernel(x_ref, g_ref, sh_ref, sc_ref, w_ref, mu_ref, w0_ref, wb_ref, a0_ref, ab_ref, gb_ref,
                     xi_ref, al_ref, rho_ref, ones_ref, tril_ref,
                     rt_ref, kt_ref, bt_ref, ktt_ref, v_ref, ktp_ref, bp_ref, pc_ref, gout_ref, bonus_ref,
                     gd_ref, xd_ref, carry_ref, *, tm, wmix, nc):
    @pl.when(pl.program_id(1) == 0)
    def _():
        carry_ref[...] = jnp.zeros(carry_ref.shape, F32)

    h = _norm_mod(x_ref[0], g_ref[...], sh_ref[0], sc_ref[0]).astype(BF16)
    pc = _dot(h, w_ref[:, 0:nc])
    pd = _dot(h, w_ref[:, nc:nc + 2 * wmix])
    gd_ref[0] = pd[:, 0:wmix]
    xd_ref[0] = pd[:, wmix:2 * wmix]

    row = _iota((tm, nc), 0)
    prev = jnp.where(row == 0, carry_ref[0:1, :], pltpu.roll(pc, 1, 0))
    carry_ref[0:1, :] = pc[tm - 1:tm, :]
    pc = pc + mu_ref[...] * (prev - pc)

    r, k, v = pc[:, 0:wmix], pc[:, wmix:2 * wmix], pc[:, 2 * wmix:3 * wmix]
    wa = pc[:, 3 * wmix:3 * wmix + LANES]
    gl = pc[:, 3 * wmix + LANES:3 * wmix + 2 * LANES]
    log_w = -math.exp(-0.5) * _sigmoid(w0_ref[...] + _dot(jnp.tanh(wa).astype(BF16), wb_ref[...]))
    a = _sigmoid(a0_ref[...] + _dot(wa.astype(BF16), ab_ref[...]))
    gout_ref[0] = _dot(_sigmoid(gl).astype(BF16), gb_ref[...])

    ones = ones_ref[...]

    def gsum(z):
        hi, lo = _split2(z)
        return jnp.concatenate([_dot(hi[:, j:j + LANES], ones) + _dot(lo[:, j:j + LANES], ones)
                                for j in range(0, wmix, LANES)], axis=1)

    kap = k * xi_ref[...]
    kap = kap * lax.rsqrt(gsum(kap * kap) + EPS)
    kt = k * (1.0 + (a - 1.0) * al_ref[...])
    bonus_ref[0] = gsum(r * rho_ref[...] * kt) * v
    bvec = a * kap

    c = RWKV_CHUNK
    cs_chunks = [_lhs_exact_dot(tril_ref[...], log_w[i:i + c]) for i in range(0, tm, c)]
    cs = jnp.concatenate(cs_chunks, axis=0)
    tot = jnp.concatenate([jnp.broadcast_to(z[c - 1:c, :], z.shape) for z in cs_chunks], axis=0)
    dec_out = jnp.exp(-cs)
    dec_end = jnp.exp(tot - cs)
    rt_ref[0] = (r * jnp.exp(cs)).astype(BF16)
    kt_ref[0] = (kap * jnp.exp(cs - log_w)).astype(BF16)
    bt_ref[0] = (bvec * dec_out).astype(BF16)
    ktt_ref[0] = (kt * dec_out).astype(BF16)
    v_ref[0] = v.astype(BF16)
    ktp_ref[0] = (kt * dec_end).astype(BF16)
    bp_ref[0] = (bvec * dec_end).astype(BF16)
    pc_ref[0] = jnp.exp(tot)


def odd_proj(x, g, shift, scale, w, mu, w0, wb, a0, ab, gb, xi, alpha, rho, *, tm):
    b, t, d = x.shape
    wmix = w0.shape[1]
    nc = mu.shape[1]
    ones = jnp.asarray(np.kron(np.eye(LANES // HEAD_DIM), np.ones((HEAD_DIM, HEAD_DIM))), BF16)
    tril = jnp.asarray(np.tril(np.ones((RWKV_CHUNK, RWKV_CHUNK))), BF16)
    row = lambda wdt: pl.BlockSpec((1, tm, wdt), lambda i, j: (i, j, 0))
    vec = pl.BlockSpec((1, 1, d), lambda i, j: (i, 0, 0))
    cvec = _const_spec((1, wmix))
    outs = [BF16] * 7 + [F32] * 5
    return pl.pallas_call(
        functools.partial(_odd_proj_kernel, tm=tm, wmix=wmix, nc=nc),
        grid=(b, t // tm),
        in_specs=[row(d), _const_spec((1, d)), vec, vec, _const_spec(w.shape), _const_spec((1, nc)),
                  cvec, _const_spec(wb.shape), cvec, _const_spec(ab.shape), _const_spec(gb.shape),
                  cvec, cvec, cvec, _const_spec(ones.shape), _const_spec(tril.shape)],
        out_specs=[row(wmix)] * 12,
        out_shape=[jax.ShapeDtypeStruct((b, t, wmix), dt) for dt in outs],
        scratch_shapes=[pltpu.VMEM((8, nc), F32)],
        compiler_params=_cp(2),
    )(x, g, shift, scale, w, mu, w0, wb, a0, ab, gb, xi, alpha, rho, ones, tril)


def _rwkv_scan_kernel(rt_ref, kt_ref, bt_ref, ktt_ref, v_ref, ktp_ref, bp_ref, pc_ref, g_ref, bonus_ref,
                      lng_ref, lnb_ref, y_ref, s_ref, *, nb, nh):
    @pl.when(pl.program_id(0) == 0)
    def _():
        s_ref[...] = jnp.zeros(s_ref.shape, F32)

    c = RWKV_CHUNK
    ti, si = _iota((2 * c, c), 0), _iota((2 * c, c), 1)
    tri = ((ti < c) & (ti > si)) | ((ti >= c) & (ti - c >= si))
    eye = jnp.where(_iota((c, c), 0) == _iota((c, c), 1), 1.0, 0.0)
    chains = [(b, h) for b in range(nb) for h in range(nh)]
    col = lambda h: slice(h * HEAD_DIM, (h + 1) * HEAD_DIM)
    ld = lambda ref: [ref[b, :, col(h)] for b, h in chains]
    kt, rt, bt, ktt, v = ld(kt_ref), ld(rt_ref), ld(bt_ref), ld(ktt_ref), ld(v_ref)
    kr = [jnp.concatenate([k_, r_], axis=0) for k_, r_ in zip(kt, rt)]
    nq = [jnp.where(tri, _dot_nt(x, y), 0.0) for x, y in zip(kr, bt)]
    aq = [jnp.where(tri, _dot_nt(x, y), 0.0).astype(BF16) for x, y in zip(kr, ktt)]
    aqv = [_dot(x, y) for x, y in zip(aq, v)]
    npow = [x[0:c] for x in nq]
    tinv = [eye - x for x in npow]
    for _ in range(int(math.log2(c)) - 1):
        npb = [x.astype(BF16) for x in npow]
        npow = [_dot(x, x) for x in npb]
        tinv = [t_ + _dot(t_.astype(BF16), p_.astype(BF16)) for t_, p_ in zip(tinv, npow)]
    s = [s_ref[i] for i in range(len(chains))]
    krs = [_dot_nt(x, s_.astype(BF16)) for x, s_ in zip(kr, s)]
    u = [_dot(t_.astype(BF16), (a_[0:c] + b_[0:c]).astype(BF16)).astype(BF16)
         for t_, a_, b_ in zip(tinv, krs, aqv)]
    y = [a_[c:2 * c] + b_[c:2 * c] - _dot(q_[c:2 * c].astype(BF16), u_)
         for a_, b_, q_, u_ in zip(krs, aqv, nq, u)]
    ktp, bp = ld(ktp_ref), ld(bp_ref)
    for i, (b, h) in enumerate(chains):
        upd = _dot_tn(jnp.concatenate([v[i], -u[i]], axis=0), jnp.concatenate([ktp[i], bp[i]], axis=0))
        s_ref[i] = s[i] * pc_ref[b, 0:1, col(h)] + upd
    for i, (b, h) in enumerate(chains):
        mu = jnp.mean(y[i], axis=1, keepdims=True)
        var = jnp.mean(jnp.square(y[i] - mu), axis=1, keepdims=True)
        yn = (y[i] - mu) * lax.rsqrt(var + EPS) * lng_ref[:, col(h)] + lnb_ref[:, col(h)]
        y_ref[b, :, col(h)] = (g_ref[b, :, col(h)] * (yn + bonus_ref[b, :, col(h)])).astype(BF16)


def rwkv_scan(rt, kt, bt, ktt, v, ktp, bp, pc, g, bonus, ln_g, ln_b):
    b, t, wmix = rt.shape
    nh = wmix // HEAD_DIM
    c = RWKV_CHUNK
    row = pl.BlockSpec((b, c, wmix), lambda j: (0, j, 0))
    cvec = _const_spec((1, wmix))
    return pl.pallas_call(
        functools.partial(_rwkv_scan_kernel, nb=b, nh=nh),
        grid=(t // c,),
        in_specs=[row] * 10 + [cvec, cvec],
        out_specs=row,
        out_shape=jax.ShapeDtypeStruct((b, t, wmix), BF16),
        scratch_shapes=[pltpu.VMEM((b * nh, HEAD_DIM, HEAD_DIM), F32)],
        compiler_params=_cp(1),
    )(rt, kt, bt, ktt, v, ktp, bp, pc, g, bonus, ln_g, ln_b)


def _lru_kernel(gd_ref, xd_ref, cw_ref, cb_ref, wa_ref, ba_ref, wx_ref, bx_ref, lam_ref, y_ref,
                xcarry_ref, hcarry_ref, *, tl, kconv):
    @pl.when(pl.program_id(1) == 0)
    def _():
        xcarry_ref[...] = jnp.zeros(xcarry_ref.shape, F32)
        hcarry_ref[...] = jnp.zeros(hcarry_ref.shape, F32)

    x = xd_ref[0]
    wdt = x.shape[1]
    row = _iota((tl, wdt), 0)
    prev = xcarry_ref[...]
    xc = cw_ref[kconv - 1:kconv, :] * x + cb_ref[...]
    for dly in range(1, kconv):
        sh = pltpu.roll(x, dly, 0)
        for r0 in range(dly):
            sh = jnp.where(row == r0, prev[8 - dly + r0:8 - dly + r0 + 1, :], sh)
        xc = xc + cw_ref[kconv - 1 - dly:kconv - dly, :] * sh
    xcarry_ref[...] = x[tl - 8:tl, :]

    xb = xc.astype(BF16)
    r = _sigmoid(_dot(xb, wa_ref[...]) + ba_ref[...])
    i = _sigmoid(_dot(xb, wx_ref[...]) + bx_ref[...])
    nl = -lam_ref[...]
    softplus = jnp.maximum(nl, 0.0) + jnp.log1p(jnp.exp(-jnp.abs(nl)))
    log_a = -LRU_C * r * softplus
    a = jnp.exp(log_a)
    th = jnp.tanh(log_a)
    bterm = jnp.sqrt(-2.0 * th / (1.0 - th)) * (i * xc)

    d = 1
    while d < tl:
        keep = row >= d
        a_sh = jnp.where(keep, pltpu.roll(a, d, 0), 1.0)
        b_sh = jnp.where(keep, pltpu.roll(bterm, d, 0), 0.0)
        bterm = a * b_sh + bterm
        a = a * a_sh
        d *= 2
    hseq = bterm + a * hcarry_ref[0:1, :]
    hcarry_ref[0:1, :] = hseq[tl - 1:tl, :]
    y_ref[0] = (_gelu(gd_ref[0]) * hseq).astype(BF16)


def rglru(gd, xd, conv_w, conv_b, wa_bd, ba, wx_bd, bx, lam, *, tl):
    b, t, wdt = xd.shape
    kconv = conv_w.shape[0]
    row = pl.BlockSpec((1, tl, wdt), lambda i, j: (i, j, 0))
    cvec = _const_spec((1, wdt))
    return pl.pallas_call(
        functools.partial(_lru_kernel, tl=tl, kconv=kconv),
        grid=(b, t // tl),
        in_specs=[row, row, _const_spec(conv_w.shape), cvec, _const_spec(wa_bd.shape), cvec,
                  _const_spec(wx_bd.shape), cvec, cvec],
        out_specs=row,
        out_shape=jax.ShapeDtypeStruct((b, t, wdt), BF16),
        scratch_shapes=[pltpu.VMEM((8, wdt), F32), pltpu.VMEM((8, wdt), F32)],
        compiler_params=_cp(2),
    )(gd, xd, conv_w, conv_b, wa_bd, ba, wx_bd, bx, lam)


def _final_norm_kernel(x_ref, g_ref, o_ref):
    x = x_ref[0]
    o_ref[0] = x * lax.rsqrt(jnp.mean(x * x, axis=-1, keepdims=True) + EPS) * g_ref[...]


def final_norm(x, g, *, tm):
    b, t, d = x.shape
    return pl.pallas_call(
        _final_norm_kernel,
        grid=(b, t // tm),
        in_specs=[pl.BlockSpec((1, tm, d), lambda i, j: (i, j, 0)), _const_spec((1, d))],
        out_specs=pl.BlockSpec((1, tm, d), lambda i, j: (i, j, 0)),
        out_shape=jax.ShapeDtypeStruct((b, t, d), F32),
        compiler_params=_cp(2),
    )(x, g.reshape(1, d))


def _block_diag(w):
    n, a, b = w.shape
    eye = jnp.eye(n, dtype=w.dtype)
    return (eye[:, None, :, None] * w[:, :, None, :]).reshape(n * a, n * b)


def _even_weights(w_in, mix_half):
    d = w_in.shape[0]
    nh = mix_half // HEAD_DIM
    hk = nh // NSA_GROUP
    nkv = hk * HEAD_DIM
    o = 0
    wq = w_in[:, o:o + mix_half].reshape(d, nh, HEAD_DIM); o += mix_half
    kvs = []
    for _ in range(6):
        kvs.append(w_in[:, o:o + nkv]); o += nkv
    wg = w_in[:, o:o + 3 * nh].reshape(d, hk, 3 * NSA_GROUP); o += 3 * nh
    wuv = w_in[:, o:]
    kv_of = jnp.arange(nh) // NSA_GROUP
    place = jax.nn.one_hot(kv_of, LANES // HEAD_DIM, dtype=w_in.dtype)
    wq_pad = (wq[:, :, None, :] * place[None, :, :, None]).reshape(d, nh * LANES)
    wg_pad = jnp.pad(wg, ((0, 0), (0, 0), (0, LANES - 3 * NSA_GROUP))).reshape(d, hk * LANES)
    return jnp.concatenate([wq_pad] + kvs + [wg_pad, wuv], axis=1).astype(BF16)


def _compress_weights(pe, w1, w2, hk):
    hid = w1.shape[1]
    pe2 = jnp.tile(pe, (1, hk))
    w1r = w1.reshape(CMP_BLOCK, HEAD_DIM, hid)
    eye = jnp.eye(hk, dtype=w1.dtype)
    w1_bd = (eye[None, :, None, :, None] * w1r[:, None, :, None, :]).reshape(CMP_BLOCK, hk * HEAD_DIM, hk * hid)
    w2_bd = (eye[:, None, :, None] * w2[None, :, None, :]).reshape(hk * hid, hk * HEAD_DIM)
    return pe2, w1_bd.astype(BF16), w2_bd.astype(BF16)


def _ffn_weights(w_up, conv_w, conv_b, w_down):
    return w_up.astype(BF16), conv_w, conv_b.reshape(1, -1), w_down.astype(BF16)


def kernel(x, c, positions, ada_w, ada_b, norm1_g, norm2_g, ev_w_in, ev_w_out, nsa_pe_k, nsa_w1_k, nsa_w2_k, nsa_pe_v, nsa_w1_v, nsa_w2_v, gmlp_ln_g, gmlp_ln_b, gmlp_ws, gmlp_bs, od_w_in, od_w_out, rwkv_mu, rwkv_w0, rwkv_wB, rwkv_a0, rwkv_aB, rwkv_gB, rwkv_xi, rwkv_alpha, rwkv_rho, rwkv_ln_g, rwkv_ln_b, lru_conv_w, lru_conv_b, lru_wa, lru_ba, lru_wx, lru_bx, lru_lambda, ffn_up, ffn_conv_w, ffn_conv_b, ffn_down, final_g):
    b, t, d = x.shape
    depth = ada_w.shape[0]
    mix_half = d // 2
    hk = mix_half // HEAD_DIM // NSA_GROUP
    assert hk * HEAD_DIM == LANES and t % 512 == 0
    tm = 512

    mod = adaln_mod(c, ada_w, ada_b)
    mods = mod.reshape(depth, b, 6, 1, d)
    tabs = rope_tables(positions, tm)
    n_cmp = (t - CMP_BLOCK) // CMP_STRIDE + 1
    ngrp = t // CMP_STRIDE
    pos_end = jnp.pad(positions[:, CMP_BLOCK - 1::CMP_STRIDE][:, :n_cmp], ((0, 0), (0, ngrp - n_cmp)))
    end_tabs = rope_tables(pos_end, ngrp)

    for layer in range(depth):
        i = layer // 2
        sh1, sc1, g1, sh2, sc2, g2 = (mods[layer, :, j] for j in range(6))
        n1 = norm1_g[layer].reshape(1, d)
        n2 = norm2_g[layer].reshape(1, d)
        if layer % 2 == 0:
            w_in = _even_weights(ev_w_in[i], mix_half)
            q, kc, vc, ks, vst, kw, vwt, gates, uv = even_proj(x, n1, sh1, sc1, w_in, tabs, tm=tm, tk_sel=512,
                                                             tk_win=256)
            pek, w1k, w2k = _compress_weights(nsa_pe_k[i], nsa_w1_k[i], nsa_w2_k[i], hk)
            pev, w1v, w2v = _compress_weights(nsa_pe_v[i], nsa_w1_v[i], nsa_w2_v[i], hk)
            kcmp, vcmp = nsa_compress(kc, vc, pek, w1k, w2k, pev, w1v, w2v, end_tabs)
            oc, sel_t = nsa_cmp_topk(q, kcmp, vcmp, tq=256)
            ya = nsa_sel_win(q, ks, vst, kw, vwt, sel_t, oc, gates, tq=256)
            yb = gmlp_gating(uv, gmlp_ln_g[i], gmlp_ln_b[i], gmlp_ws[i], gmlp_bs[i], tg=tm)
            wo = ev_w_out[i].astype(BF16)
        else:
            w_in = od_w_in[i].astype(BF16)
            lora = rwkv_wB.shape[1]
            wb = jnp.pad(rwkv_wB[i], ((0, LANES - lora), (0, 0))).astype(BF16)
            ab = jnp.pad(rwkv_aB[i], ((LANES - rwkv_aB.shape[1], 0), (0, 0))).astype(BF16)
            v1 = lambda p: p.reshape(1, -1)
            (rt, kt, bt, ktt, v, ktp, bp, pc, gout, bonus, gd, xd) = odd_proj(
                x, n1, sh1, sc1, w_in, v1(rwkv_mu[i]), v1(rwkv_w0[i]), wb, v1(rwkv_a0[i]), ab,
                rwkv_gB[i].astype(BF16), v1(rwkv_xi[i]), v1(rwkv_alpha[i]), v1(rwkv_rho[i]), tm=tm)
            ya = rwkv_scan(rt, kt, bt, ktt, v, ktp, bp, pc, gout, bonus, v1(rwkv_ln_g[i]), v1(rwkv_ln_b[i]))
            yb = rglru(gd, xd, lru_conv_w[i], v1(lru_conv_b[i]), _block_diag(lru_wa[i]).astype(BF16),
                       v1(lru_ba[i]), _block_diag(lru_wx[i]).astype(BF16), v1(lru_bx[i]), v1(lru_lambda[i]), tl=tm)
            wo = od_w_out[i].astype(BF16)
        wup, cw, cb, wdn = _ffn_weights(ffn_up[layer], ffn_conv_w[layer], ffn_conv_b[layer], ffn_down[layer])
        x = outproj_ffn(ya, yb, x, wo, g1, n2, sh2, sc2, g2, wup, cw, cb, wdn, tm=tm, cwid=256)
    return final_norm(x, final_g, tm=tm)
```

```python
import functools
import math

import numpy as np
import jax
import jax.numpy as jnp
from jax import lax
from jax.experimental import pallas as pl
from jax.experimental.pallas import tpu as pltpu

F32 = jnp.float32
BF16 = jnp.bfloat16

HEAD_DIM = 64
ROPE_DIM = HEAD_DIM // 4
ROPE_THETA = 500000.0
EPS = 1e-6
NEG_INF = -1e30
LANES = 128

NSA_GROUP = 4
CMP_BLOCK = 32
CMP_STRIDE = 16
SLC_BLOCK = 64
SLC_SHIFT = 6
SLC_TOPK = 16
WINDOW = 512
FORCE_SCORE = 1e3
MASK_BIG = -NEG_INF
LOG2E = math.log2(math.e)
GMLP_CHUNK = 128
RWKV_CHUNK = 64
LRU_C = 8.0

VMEM_LIMIT = 56 * 1024 * 1024


def _cp(n_axes, vmem=VMEM_LIMIT):
    return pltpu.CompilerParams(dimension_semantics=("arbitrary",) * n_axes, vmem_limit_bytes=vmem)


def _const_spec(shape):
    nd = len(shape)
    return pl.BlockSpec(shape, lambda *_: (0,) * nd, pipeline_mode=pl.Buffered(1))


def _dot(a, b):
    return jnp.dot(a, b, preferred_element_type=F32)


def _dot_nt(a, b):
    return lax.dot_general(a, b, (((1,), (1,)), ((), ())), preferred_element_type=F32)


def _dot_tn(a, b):
    return lax.dot_general(a, b, (((0,), (0,)), ((), ())), preferred_element_type=F32)


def _split2(z):
    hi = z.astype(BF16)
    lo = (z - hi.astype(F32)).astype(BF16)
    return hi, lo


def _split3(z):
    hi = z.astype(BF16)
    r = z - hi.astype(F32)
    mid = r.astype(BF16)
    lo = (r - mid.astype(F32)).astype(BF16)
    return hi, mid, lo


def _dot_exact_rhs(z, m):
    hi, mid, lo = _split3(z)
    return _dot(hi, m) + _dot(mid, m) + _dot(lo, m)


def _lhs_exact_dot(m, z):
    hi, mid, lo = _split3(z)
    return _dot(m, hi) + _dot(m, mid) + _dot(m, lo)


def _norm_mod(x, g, shift, scale):
    y = x * lax.rsqrt(jnp.mean(x * x, axis=-1, keepdims=True) + EPS)
    return (y * g) * (1.0 + scale) + shift


def _gelu(x):
    return jax.nn.gelu(x, approximate=True)


def _sigmoid(x):
    return jax.nn.sigmoid(x)


def _iota(shape, axis):
    return lax.broadcasted_iota(jnp.int32, shape, axis)


def _adaln_kernel(c_ref, w_ref, b_ref, o_ref):
    cond = c_ref[...]
    cond = cond * _sigmoid(cond)
    o_ref[0] = _dot(cond.astype(BF16), w_ref[0].astype(BF16)) + b_ref[0]


def adaln_mod(c, ada_w, ada_b):
    depth, d, n = ada_w.shape
    b = c.shape[0]
    rows = 16
    cp = jnp.zeros((rows, d), F32).at[:b].set(c)
    tn = 1536
    out = pl.pallas_call(
        _adaln_kernel,
        grid=(depth, n // tn),
        in_specs=[pl.BlockSpec((rows, d), lambda l, j: (0, 0)),
                  pl.BlockSpec((1, d, tn), lambda l, j: (l, 0, j)),
                  pl.BlockSpec((1, 1, tn), lambda l, j: (l, 0, j))],
        out_specs=pl.BlockSpec((1, rows, tn), lambda l, j: (l, 0, j)),
        out_shape=jax.ShapeDtypeStruct((depth, rows, n), F32),
        compiler_params=_cp(2),
    )(cp, ada_w, ada_b.reshape(depth, 1, n))
    return out[:, :b]


def _rope_kernel(pos_ref, inv_ref, c_ref, sa_ref, sb_ref):
    ang = pos_ref[0].astype(F32) * inv_ref[...]
    cos, sin = jnp.cos(ang), jnp.sin(ang)
    d = _iota(ang.shape, 1) % HEAD_DIM
    c_ref[0] = jnp.where(d < ROPE_DIM, cos, 1.0)
    sa_ref[0] = jnp.where(d < ROPE_DIM // 2, -sin, 0.0)
    sb_ref[0] = jnp.where((d >= ROPE_DIM // 2) & (d < ROPE_DIM), sin, 0.0)


def rope_tables(pos, tm):
    b, n = pos.shape
    half = ROPE_DIM // 2
    inv = ROPE_THETA ** (-jnp.arange(0, ROPE_DIM, 2, dtype=F32) / ROPE_DIM)
    inv_lane = jnp.tile(inv, LANES // half).reshape(1, LANES)
    spec = pl.BlockSpec((1, tm, LANES), lambda i, j: (i, j, 0))
    return pl.pallas_call(
        _rope_kernel,
        grid=(b, n // tm),
        in_specs=[pl.BlockSpec((1, tm, 1), lambda i, j: (i, j, 0)),
                  pl.BlockSpec((1, LANES), lambda i, j: (0, 0))],
        out_specs=[spec, spec, spec],
        out_shape=[jax.ShapeDtypeStruct((b, n, LANES), F32)] * 3,
        compiler_params=_cp(2),
    )(pos[..., None], inv_lane)


def _rope(y, c, sa, sb):
    return y * c + pltpu.roll(y, LANES - ROPE_DIM // 2, 1) * sa + pltpu.roll(y, ROPE_DIM // 2, 1) * sb


def _even_proj_kernel(x_ref, g_ref, sh_ref, sc_ref, w_ref, c_ref, sa_ref, sb_ref,
                      q_ref, kc_ref, vc_ref, ks_ref, vst_ref, kw_ref, vwt_ref, gate_ref, uv_ref,
                      *, nq, nuv):
    h = _norm_mod(x_ref[0], g_ref[...], sh_ref[0], sc_ref[0]).astype(BF16)
    c, sa, sb = c_ref[0], sa_ref[0], sb_ref[0]
    scale = HEAD_DIM ** -0.5 * LOG2E
    yq = _dot(h, w_ref[:, 0:nq])
    for j in range(nq // LANES):
        sl = slice(j * LANES, (j + 1) * LANES)
        q_ref[0, :, sl] = (_rope(yq[:, sl], c, sa, sb) * scale).astype(BF16)
    o = nq
    ykv = _dot(h, w_ref[:, o:o + 6 * LANES])
    kc_ref[0] = ykv[:, 0:LANES]
    vc_ref[0] = ykv[:, LANES:2 * LANES]
    tm = ykv.shape[0]
    blk = jnp.right_shift(pl.program_id(1) * tm + _iota((tm, LANES), 0), SLC_SHIFT)
    ks_ref[0, :, 0:LANES] = _rope(ykv[:, 2 * LANES:3 * LANES], c, sa, sb).astype(BF16)
    ks_ref[0, :, LANES:2 * LANES] = jnp.where(_iota((tm, LANES), 1) == blk, MASK_BIG, 0.0).astype(BF16)
    kw_ref[0] = _rope(ykv[:, 4 * LANES:5 * LANES], c, sa, sb).astype(BF16)
    for ref, lo in ((vst_ref, 3 * LANES), (vwt_ref, 5 * LANES)):
        tile = ref.shape[3]
        for i in range(ref.shape[1]):
            ref[0, i] = ykv[i * tile:(i + 1) * tile, lo:lo + LANES].T.astype(BF16)
    o += 6 * LANES
    gate_ref[0] = _sigmoid(_dot(h, w_ref[:, o:o + 2 * LANES]))
    o += 2 * LANES
    uv_ref[0] = _dot(h, w_ref[:, o:o + nuv])


def even_proj(x, g, shift, scale, w, tabs, *, tm, tk_sel, tk_win):
    b, t, d = x.shape
    n = w.shape[1]
    nq = 2 * NSA_GROUP * LANES
    nuv = n - nq - 8 * LANES
    row = lambda wdt: pl.BlockSpec((1, tm, wdt), lambda i, j: (i, j, 0))
    vec = pl.BlockSpec((1, 1, d), lambda i, j: (i, 0, 0))
    rows = lambda wdt, dt: (row(wdt), jax.ShapeDtypeStruct((b, t, wdt), dt))
    vt = lambda tk: (pl.BlockSpec((1, tm // tk, LANES, tk), lambda i, j: (i, j, 0, 0)),
                     jax.ShapeDtypeStruct((b, t // tk, LANES, tk), BF16))
    outs = [rows(nq, BF16), rows(LANES, F32), rows(LANES, F32), rows(2 * LANES, BF16), vt(tk_sel),
            rows(LANES, BF16), vt(tk_win), rows(2 * LANES, F32), rows(nuv, F32)]
    return pl.pallas_call(
        functools.partial(_even_proj_kernel, nq=nq, nuv=nuv),
        grid=(b, t // tm),
        in_specs=[row(d), _const_spec((1, d)), vec, vec, _const_spec((d, n)),
                  row(LANES), row(LANES), row(LANES)],
        out_specs=[s for s, _ in outs],
        out_shape=[s for _, s in outs],
        compiler_params=_cp(2),
    )(x, g, shift, scale, w, *tabs)


def _compress_kernel(kc_ref, vc_ref, pek_ref, w1k_ref, w2k_ref, pev_ref, w1v_ref, w2v_ref,
                     c_ref, sa_ref, sb_ref, ko_ref, vo_ref, *, ngrp, n_cmp):
    half = CMP_BLOCK // 2

    def phi(src_ref, pe_ref, w1_ref, w2_ref):
        p = jnp.zeros((ngrp, w1_ref.shape[2]), F32)
        q = jnp.zeros((ngrp, w1_ref.shape[2]), F32)
        for l in range(half):
            a = src_ref[0, pl.ds(l, ngrp, stride=CMP_STRIDE), :]
            p = p + _dot((a + pe_ref[l:l + 1, :]).astype(BF16), w1_ref[l])
            q = q + _dot((a + pe_ref[half + l:half + l + 1, :]).astype(BF16), w1_ref[half + l])
        hid = p + pltpu.roll(q, ngrp - 1, 0)
        return _dot(_gelu(hid).astype(BF16), w2_ref[...])

    row = _iota((ngrp, LANES), 0)
    kc = _rope(phi(kc_ref, pek_ref, w1k_ref, w2k_ref), c_ref[0], sa_ref[0], sb_ref[0])
    ko_ref[0] = jnp.where(row < n_cmp, kc, 0.0).astype(BF16)
    vc = phi(vc_ref, pev_ref, w1v_ref, w2v_ref)
    vo_ref[0] = jnp.where(row < n_cmp, vc, 0.0).T.astype(BF16)


def nsa_compress(kc, vc, pek, w1k, w2k, pev, w1v, w2v, end_tabs):
    b, t, _ = kc.shape
    ngrp = t // CMP_STRIDE
    n_cmp = (t - CMP_BLOCK) // CMP_STRIDE + 1
    tok = pl.BlockSpec((1, t, LANES), lambda i: (i, 0, 0))
    grp = pl.BlockSpec((1, ngrp, LANES), lambda i: (i, 0, 0))
    return pl.pallas_call(
        functools.partial(_compress_kernel, ngrp=ngrp, n_cmp=n_cmp),
        grid=(b,),
        in_specs=[tok, tok, _const_spec(pek.shape), _const_spec(w1k.shape), _const_spec(w2k.shape),
                  _const_spec(pev.shape), _const_spec(w1v.shape), _const_spec(w2v.shape), grp, grp, grp],
        out_specs=[grp, pl.BlockSpec((1, LANES, ngrp), lambda i: (i, 0, 0))],
        out_shape=[jax.ShapeDtypeStruct((b, ngrp, LANES), BF16), jax.ShapeDtypeStruct((b, LANES, ngrp), BF16)],
        compiler_params=_cp(1),
    )(kc, vc, pek, w1k, w2k, pev, w1v, w2v, *end_tabs)


def _cmp_topk_kernel(q_ref, kc_ref, vc_ref, ovt_ref, oc_ref, sel_ref, *, tq, n_cmp, ncp, n_slc, k_top):
    t0 = pl.program_id(2) * tq
    kc, vct = kc_ref[0], vc_ref[0]
    end = CMP_BLOCK - 1
    n_row, t_col = _iota((ncp, tq), 0), t0 + _iota((ncp, tq), 1)
    valid_t = (n_row * CMP_STRIDE + end <= t_col) & (n_row < n_cmp)
    psum_t = jnp.zeros((ncp, tq), F32)
    for g in range(NSA_GROUP):
        qg = q_ref[0, :, g * LANES:(g + 1) * LANES]
        st = jnp.where(valid_t, _dot_nt(kc, qg), NEG_INF)
        et = jnp.where(valid_t, jnp.exp2(st - jnp.max(st, axis=0, keepdims=True)), 0.0)
        lt = jnp.sum(et, axis=0, keepdims=True)
        pt = et * (1.0 / jnp.where(lt > 0.0, lt, 1.0))
        oc_ref[0, g * LANES:(g + 1) * LANES, :] = _dot(vct, pt.astype(BF16))
        psum_t = psum_t + pt
    hi, lo = _split2(psum_t)
    imp = _dot(ovt_ref[...], hi) + _dot(ovt_ref[...], lo)
    j = _iota((n_slc, tq), 0)
    t = t0 + _iota((n_slc, tq), 1)
    cur = jnp.right_shift(t, SLC_SHIFT)
    forced = (j == 0) | (j == cur) | (j == cur - 1)
    score = jnp.where(forced, FORCE_SCORE, jnp.where(j * SLC_BLOCK <= t, imp, -1.0))
    rank = jnp.zeros((n_slc, tq), jnp.int32)
    for jj in range(n_slc):
        r = score[jj:jj + 1, :]
        beats = (r > score) | ((r == score) & (j > jj))
        rank = rank + beats.astype(jnp.int32)
    sel_ref[0, 0] = jnp.where(rank < k_top, 1.0, 0.0).astype(BF16)


def nsa_cmp_topk(q, kcmp, vcmp, *, tq):
    b, t, nq = q.shape
    hk = nq // (NSA_GROUP * LANES)
    ncp = kcmp.shape[1]
    n_cmp = (t - CMP_BLOCK) // CMP_STRIDE + 1
    n_slc = t // SLC_BLOCK
    cs = np.arange(ncp) * CMP_STRIDE
    ss = np.arange(n_slc) * SLC_BLOCK
    ov = np.clip(np.minimum(cs[:, None] + CMP_BLOCK, ss[None, :] + SLC_BLOCK)
                 - np.maximum(cs[:, None], ss[None, :]), 0, None).astype(np.float32) / CMP_BLOCK
    ov[n_cmp:] = 0.0
    ovt = jnp.asarray(ov.T, BF16)
    gq = NSA_GROUP * LANES
    return pl.pallas_call(
        functools.partial(_cmp_topk_kernel, tq=tq, n_cmp=n_cmp, ncp=ncp, n_slc=n_slc,
                          k_top=min(SLC_TOPK, n_slc)),
        grid=(b, hk, t // tq),
        in_specs=[pl.BlockSpec((1, tq, gq), lambda i, k, j: (i, j, k)),
                  pl.BlockSpec((1, ncp, LANES), lambda i, k, j: (i, 0, 0)),
                  pl.BlockSpec((1, LANES, ncp), lambda i, k, j: (i, 0, 0)),
                  _const_spec((n_slc, ncp))],
        out_specs=[pl.BlockSpec((1, gq, tq), lambda i, k, j: (i, k, j)),
                   pl.BlockSpec((1, 1, n_slc, tq), lambda i, k, j: (i, k, 0, j))],
        out_shape=[jax.ShapeDtypeStruct((b, nq, t), F32),
                   jax.ShapeDtypeStruct((b, hk, n_slc, t), BF16)],
        compiler_params=_cp(3),
    )(q, kcmp, vcmp, ovt)


def _softmax_rows(s_ref, p_ref, g, m_old, rb):
    rows, cols = s_ref.shape[1], s_ref.shape[2]
    fold = lambda z, op: op(z.reshape(rb // 8, 8, cols), axis=0)
    mx = fold(s_ref[g, 0:rb, :], jnp.max)
    for r in range(rb, rows, rb):
        mx = jnp.maximum(mx, fold(s_ref[g, r:r + rb, :], jnp.max))
    m_new = jnp.max(mx, axis=0, keepdims=True)
    if m_old is not None:
        m_new = jnp.maximum(m_old, m_new)
    acc = None
    for r in range(0, rows, rb):
        p = jnp.exp2(s_ref[g, r:r + rb, :] - m_new)
        part = fold(p, jnp.sum)
        acc = part if acc is None else acc + part
        p_ref[g, r:r + rb, :] = p.astype(BF16)
    return m_new, jnp.sum(acc, axis=0, keepdims=True)


def _sel_win_kernel(q_ref, ks_ref, vst_ref, kw_ref, vwt_ref, sel_ref, oc_ref, gate_ref, y_ref,
                    s_ref, p_ref, m_ref, l_ref, acc_ref, sw_ref, pw_ref, *, tq, tk, n_slc):
    kv = pl.program_id(1)
    qi = pl.program_id(2)
    q0 = qi * tq
    grp = NSA_GROUP
    rb = 64
    qt = [q_ref[0, :, g * LANES:(g + 1) * LANES].astype(F32).T.astype(BF16) for g in range(grp)]
    unsel = (sel_ref[0, 0].astype(F32) - 1.0).astype(BF16)
    unsel = jnp.concatenate([unsel, jnp.zeros((LANES - n_slc, tq), BF16)], axis=0)
    qa = [jnp.concatenate([z, unsel], axis=0) for z in qt]

    m_ref[...] = jnp.full(m_ref.shape, NEG_INF, F32)
    l_ref[...] = jnp.zeros(l_ref.shape, F32)
    acc_ref[...] = jnp.zeros(acc_ref.shape, F32)

    def score_tile(kt, slot, diagonal=False):
        k0 = pl.multiple_of(kt * tk, tk)
        for g in range(grp):
            s = _dot(ks_ref[0, pl.ds(k0, tk), :], qa[g])
            if diagonal:
                causal = k0 + _iota((tk, tq), 0) <= q0 + _iota((tk, tq), 1)
                s = s + jnp.where(causal, 0.0, NEG_INF)
            s_ref[slot * grp + g] = s

    def update_tile(kt, slot):
        for g in range(grp):
            m_old = m_ref[g]
            m_new, l_add = _softmax_rows(s_ref, p_ref, slot * grp + g, m_old, rb)
            alpha = jnp.exp2(m_old - m_new)
            m_ref[g] = m_new
            l_ref[g] = alpha * l_ref[g] + l_add
            acc_ref[g] = alpha * acc_ref[g] + _dot(vst_ref[0, kt], p_ref[slot * grp + g])

    kt_diag = q0 // tk
    pairs = kt_diag // 2
    odd = kt_diag % 2 == 1
    score_tile(kt_diag, 0, diagonal=True)

    def sel_body(j, carry):
        score_tile(2 * j, 1)
        update_tile(jnp.where(j == 0, kt_diag, 2 * j - 1), 0)
        score_tile(2 * j + 1, 0)
        update_tile(2 * j, 1)
        return carry

    lax.fori_loop(0, pairs, sel_body, 0)

    @pl.when(odd)
    def _():
        score_tile(kt_diag - 1, 1)

    nwt = WINDOW // tq + 1
    kt0 = jnp.maximum(qi - WINDOW // tq, 0)
    k0 = pl.multiple_of(kt0 * tq, tq)
    kp, tt = k0 + _iota((nwt * tq, tq), 0), q0 + _iota((nwt * tq, tq), 1)
    bias_w = jnp.where((kp <= tt) & (kp > tt - WINDOW), 0.0, NEG_INF)
    for g in range(grp):
        sw_ref[g] = _dot(kw_ref[0, pl.ds(k0, nwt * tq), :], qt[g]) + bias_w

    update_tile(jnp.where(pairs == 0, kt_diag, 2 * pairs - 1), 0)

    @pl.when(odd)
    def _():
        update_tile(kt_diag - 1, 1)

    o_w = []
    for g in range(grp):
        _, l_w = _softmax_rows(sw_ref, pw_ref, g, None, rb)
        acc_w = _dot(vwt_ref[0, kt0], pw_ref[g, 0:tq, :])
        for i in range(1, nwt):
            acc_w = acc_w + _dot(vwt_ref[0, kt0 + i], pw_ref[g, i * tq:(i + 1) * tq, :])
        o_w.append(acc_w / l_w)

    gate_t = gate_ref[0].T
    ys = []
    for g in range(NSA_GROUP):
        y_t = (gate_t[3 * g:3 * g + 1, :] * oc_ref[0, g * LANES:(g + 1) * LANES, :]
               + gate_t[3 * g + 1:3 * g + 2, :] * (acc_ref[g] / l_ref[g])
               + gate_t[3 * g + 2:3 * g + 3, :] * o_w[g])
        ys.append(y_t.T)
    low = _iota((tq, LANES), 1) < HEAD_DIM
    first_kv = kv == 0
    for j in range(NSA_GROUP // 2):
        a, b2 = ys[2 * j], ys[2 * j + 1]
        lo_half = jnp.where(first_kv, a, pltpu.roll(a, HEAD_DIM, 1))
        hi_half = jnp.where(first_kv, pltpu.roll(b2, HEAD_DIM, 1), b2)
        y_ref[0, :, j * LANES:(j + 1) * LANES] = jnp.where(low, lo_half, hi_half).astype(BF16)


def nsa_sel_win(q, ks, vst, kw, vwt, sel_t, oc, gates, *, tq):
    b, t, nq = q.shape
    hk = nq // (NSA_GROUP * LANES)
    n_slc = t // SLC_BLOCK
    tk = vst.shape[3]
    assert vwt.shape[3] == tq and tk % tq == 0
    gq = NSA_GROUP * LANES
    full = pl.BlockSpec((1, t, LANES), lambda i, k, j: (i, 0, 0))
    tiles = lambda a: pl.BlockSpec((1,) + a.shape[1:], lambda i, k, j: (i, 0, 0, 0))
    qspec = pl.BlockSpec((1, tq, gq), lambda i, k, j: (i, j, k))
    return pl.pallas_call(
        functools.partial(_sel_win_kernel, tq=tq, tk=tk, n_slc=n_slc),
        grid=(b, hk, t // tq),
        in_specs=[qspec, pl.BlockSpec((1, t, 2 * LANES), lambda i, k, j: (i, 0, 0)), tiles(vst), full, tiles(vwt),
                  pl.BlockSpec((1, 1, n_slc, tq), lambda i, k, j: (i, k, 0, j)),
                  pl.BlockSpec((1, gq, tq), lambda i, k, j: (i, k, j)),
                  pl.BlockSpec((1, tq, LANES), lambda i, k, j: (i, j, k))],
        out_specs=pl.BlockSpec((1, tq, NSA_GROUP * HEAD_DIM), lambda i, k, j: (i, j, k)),
        out_shape=jax.ShapeDtypeStruct((b, t, hk * NSA_GROUP * HEAD_DIM), BF16),
        scratch_shapes=[pltpu.VMEM((2 * NSA_GROUP, tk, tq), F32), pltpu.VMEM((2 * NSA_GROUP, tk, tq), BF16),
                        pltpu.VMEM((NSA_GROUP, 1, tq), F32), pltpu.VMEM((NSA_GROUP, 1, tq), F32),
                        pltpu.VMEM((NSA_GROUP, LANES, tq), F32),
                        pltpu.VMEM((NSA_GROUP, WINDOW + tq, tq), F32),
                        pltpu.VMEM((NSA_GROUP, WINDOW + tq, tq), BF16)],
        compiler_params=_cp(3),
    )(q, ks, vst, kw, vwt, sel_t, oc, gates)


def _gmlp_kernel(u_ref, v_ref, g_ref, b_ref, avg_ref, ws_ref, bs_ref, y_ref, *, tg):
    u = _gelu(u_ref[0])
    v = _gelu(v_ref[0])
    avg = avg_ref[...]

    def gmean(z):
        hi, lo = _split2(z)
        return _dot(hi, avg) + _dot(lo, avg)

    d = v - gmean(v)
    vn = (d * lax.rsqrt(gmean(d * d) + EPS) * g_ref[...] + b_ref[...]).astype(BF16)
    c = GMLP_CHUNK
    causal = _iota((c, c), 0) >= _iota((c, c), 1)
    low = _iota((c, LANES), 1) < HEAD_DIM
    width = u.shape[1]
    for j in range(width // LANES):
        w0 = jnp.where(causal, ws_ref[2 * j], 0.0).astype(BF16)
        w1 = jnp.where(causal, ws_ref[2 * j + 1], 0.0).astype(BF16)
        cols = slice(j * LANES, (j + 1) * LANES)
        for ci in range(tg // c):
            rows = slice(ci * c, (ci + 1) * c)
            v2 = vn[rows, cols]
            mixed = jnp.where(low, _dot(w0, v2), _dot(w1, v2)) + bs_ref[:, cols]
            y_ref[0, rows, cols] = (u[rows, cols] * mixed).astype(BF16)


def gmlp_gating(uv, ln_g, ln_b, ws, bs, *, tg):
    b, t, w2 = uv.shape
    w = w2 // 2
    ngrp = w // HEAD_DIM
    avg = jnp.asarray(np.kron(np.eye(ngrp), np.full((HEAD_DIM, HEAD_DIM), 1.0 / HEAD_DIM)), BF16)
    bs_exp = jnp.repeat(bs.T, HEAD_DIM, axis=1)
    return pl.pallas_call(
        functools.partial(_gmlp_kernel, tg=tg),
        grid=(b, t // tg),
        in_specs=[pl.BlockSpec((1, tg, w), lambda i, j: (i, j, 0)),
                  pl.BlockSpec((1, tg, w), lambda i, j: (i, j, 1)),
                  _const_spec((1, w)), _const_spec((1, w)), _const_spec((w, w)),
                  _const_spec(ws.shape), _const_spec((GMLP_CHUNK, w))],
        out_specs=pl.BlockSpec((1, tg, w), lambda i, j: (i, j, 0)),
        out_shape=jax.ShapeDtypeStruct((b, t, w), BF16),
        compiler_params=_cp(2),
    )(uv, uv, ln_g.reshape(1, w), ln_b.reshape(1, w), avg, ws, bs_exp)


def _outproj_ffn_kernel(ya_ref, yb_ref, x_ref, wo_ref, g1_ref, n2_ref, sh_ref, sc_ref, g2_ref,
                        wup_ref, cw_ref, cb_ref, wdn_ref, fin_ref, o_ref, carry_ref, h_ref, up_ref, act_ref,
                        *, tm, nch, cwid, rs, last_layer):
    @pl.when(pl.program_id(1) == 0)
    def _():
        carry_ref[...] = jnp.zeros(carry_ref.shape, F32)

    half = ya_ref.shape[2]
    y = _dot(ya_ref[0], wo_ref[0:half, :]) + _dot(yb_ref[0], wo_ref[half:2 * half, :])
    x1 = x_ref[0] + g1_ref[0] * y
    o_ref[0] = x1
    h_ref[...] = _norm_mod(x1, n2_ref[...], sh_ref[0], sc_ref[0]).astype(BF16)
    kconv = cw_ref.shape[0]
    cols = lambda idx: slice(idx * cwid, (idx + 1) * cwid)

    def conv_rows(slot, half, idx, r0):
        w = cw_ref[:, cols(idx)]
        out = cb_ref[:, cols(idx)]
        for d in range(kconv):
            out = out + w[kconv - 1 - d:kconv - d, :] * up_ref[slot, half, r0 + 8 - d:r0 + 8 - d + rs, :]
        return out

    for c in range(nch):
        slot = c % 2
        for half, idx in ((0, c), (1, nch + c)):
            up_ref[slot, half, 0:8, :] = carry_ref[idx]
            up_ref[slot, half, 8:tm + 8, :] = _dot(h_ref[...], wup_ref[:, cols(idx)])
            carry_ref[idx] = up_ref[slot, half, tm:tm + 8, :]
        for r0 in range(0, tm, rs):
            a = conv_rows(slot, 0, c, r0)
            act = a * _sigmoid(a) * conv_rows(slot, 1, nch + c, r0)
            act_ref[r0:r0 + rs, cols(c)] = act.astype(BF16)
    x2 = o_ref[0] + g2_ref[0] * _dot(act_ref[...], wdn_ref[...])
    if last_layer:
        x2 = x2 * lax.rsqrt(jnp.mean(x2 * x2, axis=-1, keepdims=True) + EPS) * fin_ref[...]
    o_ref[0] = x2


def outproj_ffn(ya, yb, x, wo, g1, n2g, sh2, sc2, g2, wup, cw, cb, wdn, final_g, *, tm, cwid, last_layer):
    b, t, d = x.shape
    half = ya.shape[2]
    nch = wdn.shape[0] // cwid
    assert nch * cwid == wdn.shape[0] and wup.shape[1] == 2 * nch * cwid
    row = lambda wdt: pl.BlockSpec((1, tm, wdt), lambda i, j: (i, j, 0))
    vec = pl.BlockSpec((1, 1, d), lambda i, j: (i, 0, 0))
    return pl.pallas_call(
        functools.partial(_outproj_ffn_kernel, tm=tm, nch=nch, cwid=cwid, rs=64, last_layer=last_layer),
        grid=(b, t // tm),
        in_specs=[row(half), row(half), row(d), _const_spec(wo.shape), vec, _const_spec((1, d)), vec, vec, vec,
                  _const_spec(wup.shape), _const_spec(cw.shape), _const_spec(cb.shape), _const_spec(wdn.shape),
                  _const_spec((1, d))],
        out_specs=row(d),
        out_shape=jax.ShapeDtypeStruct((b, t, d), F32),
        scratch_shapes=[pltpu.VMEM((2 * nch, 8, cwid), F32), pltpu.VMEM((tm, d), BF16),
                        pltpu.VMEM((2, 2, tm + 8, cwid), F32), pltpu.VMEM((tm, nch * cwid), BF16)],
        compiler_params=_cp(2),
    )(ya, yb, x, wo, g1, n2g, sh2, sc2, g2, wup, cw, cb, wdn, final_g)


def _odd_proj_kernel(x_ref, g_ref, sh_ref, sc_ref, w_ref, mu_ref, w0_ref, wb_ref, a0_ref, ab_ref, gb_ref,
                     xi_ref, al_ref, rho_ref, ones_ref, tril_ref,
                     rt_ref, kt_ref, bt_ref, ktt_ref, v_ref, ktp_ref, bp_ref, pc_ref, gout_ref, bonus_ref,
                     gd_ref, xd_ref, carry_ref, *, tm, wmix, nc):
    @pl.when(pl.program_id(1) == 0)
    def _():
        carry_ref[...] = jnp.zeros(carry_ref.shape, F32)

    h = _norm_mod(x_ref[0], g_ref[...], sh_ref[0], sc_ref[0]).astype(BF16)
    pc = _dot(h, w_ref[:, 0:nc])
    pd = _dot(h, w_ref[:, nc:nc + 2 * wmix])
    gd_ref[0] = pd[:, 0:wmix]
    xd_ref[0] = pd[:, wmix:2 * wmix]

    row = _iota((tm, nc), 0)
    prev = jnp.where(row == 0, carry_ref[0:1, :], pltpu.roll(pc, 1, 0))
    carry_ref[0:1, :] = pc[tm - 1:tm, :]
    pc = pc + mu_ref[...] * (prev - pc)

    r, k, v = pc[:, 0:wmix], pc[:, wmix:2 * wmix], pc[:, 2 * wmix:3 * wmix]
    wa = pc[:, 3 * wmix:3 * wmix + LANES]
    gl = pc[:, 3 * wmix + LANES:3 * wmix + 2 * LANES]
    log_w = -math.exp(-0.5) * _sigmoid(w0_ref[...] + _dot(jnp.tanh(wa).astype(BF16), wb_ref[...]))
    a = _sigmoid(a0_ref[...] + _dot(wa.astype(BF16), ab_ref[...]))
    gout_ref[0] = _dot(_sigmoid(gl).astype(BF16), gb_ref[...])

    ones = ones_ref[...]

    def gsum(z):
        hi, lo = _split2(z)
        return jnp.concatenate([_dot(hi[:, j:j + LANES], ones) + _dot(lo[:, j:j + LANES], ones)
                                for j in range(0, wmix, LANES)], axis=1)

    kap = k * xi_ref[...]
    kap = kap * lax.rsqrt(gsum(kap * kap) + EPS)
    kt = k * (1.0 + (a - 1.0) * al_ref[...])
    bonus_ref[0] = gsum(r * rho_ref[...] * kt) * v
    bvec = a * kap

    c = RWKV_CHUNK
    cs_chunks = [_lhs_exact_dot(tril_ref[...], log_w[i:i + c]) for i in range(0, tm, c)]
    cs = jnp.concatenate(cs_chunks, axis=0)
    tot = jnp.concatenate([jnp.broadcast_to(z[c - 1:c, :], z.shape) for z in cs_chunks], axis=0)
    dec_out = jnp.exp(-cs)
    dec_end = jnp.exp(tot - cs)
    rt_ref[0] = (r * jnp.exp(cs)).astype(BF16)
    kt_ref[0] = (kap * jnp.exp(cs - log_w)).astype(BF16)
    bt_ref[0] = (bvec * dec_out).astype(BF16)
    ktt_ref[0] = (kt * dec_out).astype(BF16)
    v_ref[0] = v.astype(BF16)
    ktp_ref[0] = (kt * dec_end).astype(BF16)
    bp_ref[0] = (bvec * dec_end).astype(BF16)
    pc_ref[0] = jnp.exp(tot)


def odd_proj(x, g, shift, scale, w, mu, w0, wb, a0, ab, gb, xi, alpha, rho, *, tm):
    b, t, d = x.shape
    wmix = w0.shape[1]
    nc = mu.shape[1]
    ones = jnp.asarray(np.kron(np.eye(LANES // HEAD_DIM), np.ones((HEAD_DIM, HEAD_DIM))), BF16)
    tril = jnp.asarray(np.tril(np.ones((RWKV_CHUNK, RWKV_CHUNK))), BF16)
    row = lambda wdt: pl.BlockSpec((1, tm, wdt), lambda i, j: (i, j, 0))
    vec = pl.BlockSpec((1, 1, d), lambda i, j: (i, 0, 0))
    cvec = _const_spec((1, wmix))
    outs = [BF16] * 7 + [F32] * 5
    return pl.pallas_call(
        functools.partial(_odd_proj_kernel, tm=tm, wmix=wmix, nc=nc),
        grid=(b, t // tm),
        in_specs=[row(d), _const_spec((1, d)), vec, vec, _const_spec(w.shape), _const_spec((1, nc)),
                  cvec, _const_spec(wb.shape), cvec, _const_spec(ab.shape), _const_spec(gb.shape),
                  cvec, cvec, cvec, _const_spec(ones.shape), _const_spec(tril.shape)],
        out_specs=[row(wmix)] * 12,
        out_shape=[jax.ShapeDtypeStruct((b, t, wmix), dt) for dt in outs],
        scratch_shapes=[pltpu.VMEM((8, nc), F32)],
        compiler_params=_cp(2),
    )(x, g, shift, scale, w, mu, w0, wb, a0, ab, gb, xi, alpha, rho, ones, tril)


def _rwkv_scan_kernel(rt_ref, kt_ref, bt_ref, ktt_ref, v_ref, ktp_ref, bp_ref, pc_ref, g_ref, bonus_ref,
                      lng_ref, lnb_ref, y_ref, s_ref, *, nb, nh):
    @pl.when(pl.program_id(0) == 0)
    def _():
        s_ref[...] = jnp.zeros(s_ref.shape, F32)

    c = RWKV_CHUNK
    ti, si = _iota((2 * c, c), 0), _iota((2 * c, c), 1)
    tri = ((ti < c) & (ti > si)) | ((ti >= c) & (ti - c >= si))
    eye = jnp.where(_iota((c, c), 0) == _iota((c, c), 1), 1.0, 0.0)
    chains = [(b, h) for b in range(nb) for h in range(nh)]
    col = lambda h: slice(h * HEAD_DIM, (h + 1) * HEAD_DIM)
    ld = lambda ref: [ref[b, :, col(h)] for b, h in chains]
    kt, rt, bt, ktt, v = ld(kt_ref), ld(rt_ref), ld(bt_ref), ld(ktt_ref), ld(v_ref)
    kr = [jnp.concatenate([k_, r_], axis=0) for k_, r_ in zip(kt, rt)]
    nq = [jnp.where(tri, _dot_nt(x, y), 0.0) for x, y in zip(kr, bt)]
    aq = [jnp.where(tri, _dot_nt(x, y), 0.0).astype(BF16) for x, y in zip(kr, ktt)]
    aqv = [_dot(x, y) for x, y in zip(aq, v)]
    npow = [x[0:c] for x in nq]
    tinv = [eye - x for x in npow]
    for _ in range(int(math.log2(c)) - 1):
        npb = [x.astype(BF16) for x in npow]
        npow = [_dot(x, x) for x in npb]
        tinv = [t_ + _dot(t_.astype(BF16), p_.astype(BF16)) for t_, p_ in zip(tinv, npow)]
    s = [s_ref[i] for i in range(len(chains))]
    krs = [_dot_nt(x, s_.astype(BF16)) for x, s_ in zip(kr, s)]
    u = [_dot(t_.astype(BF16), (a_[0:c] + b_[0:c]).astype(BF16)).astype(BF16)
         for t_, a_, b_ in zip(tinv, krs, aqv)]
    y = [a_[c:2 * c] + b_[c:2 * c] - _dot(q_[c:2 * c].astype(BF16), u_)
         for a_, b_, q_, u_ in zip(krs, aqv, nq, u)]
    ktp, bp = ld(ktp_ref), ld(bp_ref)
    for i, (b, h) in enumerate(chains):
        upd = _dot_tn(jnp.concatenate([v[i], -u[i]], axis=0), jnp.concatenate([ktp[i], bp[i]], axis=0))
        s_ref[i] = s[i] * pc_ref[b, 0:1, col(h)] + upd
    for i, (b, h) in enumerate(chains):
        mu = jnp.mean(y[i], axis=1, keepdims=True)
        var = jnp.mean(jnp.square(y[i] - mu), axis=1, keepdims=True)
        yn = (y[i] - mu) * lax.rsqrt(var + EPS) * lng_ref[:, col(h)] + lnb_ref[:, col(h)]
        y_ref[b, :, col(h)] = (g_ref[b, :, col(h)] * (yn + bonus_ref[b, :, col(h)])).astype(BF16)


def rwkv_scan(rt, kt, bt, ktt, v, ktp, bp, pc, g, bonus, ln_g, ln_b):
    b, t, wmix = rt.shape
    nh = wmix // HEAD_DIM
    c = RWKV_CHUNK
    row = pl.BlockSpec((b, c, wmix), lambda j: (0, j, 0))
    cvec = _const_spec((1, wmix))
    return pl.pallas_call(
        functools.partial(_rwkv_scan_kernel, nb=b, nh=nh),
        grid=(t // c,),
        in_specs=[row] * 10 + [cvec, cvec],
        out_specs=row,
        out_shape=jax.ShapeDtypeStruct((b, t, wmix), BF16),
        scratch_shapes=[pltpu.VMEM((b * nh, HEAD_DIM, HEAD_DIM), F32)],
        compiler_params=_cp(1),
    )(rt, kt, bt, ktt, v, ktp, bp, pc, g, bonus, ln_g, ln_b)


def _lru_kernel(gd_ref, xd_ref, cw_ref, cb_ref, wa_ref, ba_ref, wx_ref, bx_ref, lam_ref, y_ref,
                xcarry_ref, hcarry_ref, *, tl, kconv):
    @pl.when(pl.program_id(1) == 0)
    def _():
        xcarry_ref[...] = jnp.zeros(xcarry_ref.shape, F32)
        hcarry_ref[...] = jnp.zeros(hcarry_ref.shape, F32)

    x = xd_ref[0]
    wdt = x.shape[1]
    row = _iota((tl, wdt), 0)
    prev = xcarry_ref[...]
    xc = cw_ref[kconv - 1:kconv, :] * x + cb_ref[...]
    for dly in range(1, kconv):
        sh = pltpu.roll(x, dly, 0)
        for r0 in range(dly):
            sh = jnp.where(row == r0, prev[8 - dly + r0:8 - dly + r0 + 1, :], sh)
        xc = xc + cw_ref[kconv - 1 - dly:kconv - dly, :] * sh
    xcarry_ref[...] = x[tl - 8:tl, :]

    xb = xc.astype(BF16)
    r = _sigmoid(_dot(xb, wa_ref[...]) + ba_ref[...])
    i = _sigmoid(_dot(xb, wx_ref[...]) + bx_ref[...])
    nl = -lam_ref[...]
    softplus = jnp.maximum(nl, 0.0) + jnp.log1p(jnp.exp(-jnp.abs(nl)))
    log_a = -LRU_C * r * softplus
    a = jnp.exp(log_a)
    th = jnp.tanh(log_a)
    bterm = jnp.sqrt(-2.0 * th / (1.0 - th)) * (i * xc)

    d = 1
    while d < tl:
        keep = row >= d
        a_sh = jnp.where(keep, pltpu.roll(a, d, 0), 1.0)
        b_sh = jnp.where(keep, pltpu.roll(bterm, d, 0), 0.0)
        bterm = a * b_sh + bterm
        a = a * a_sh
        d *= 2
    hseq = bterm + a * hcarry_ref[0:1, :]
    hcarry_ref[0:1, :] = hseq[tl - 1:tl, :]
    y_ref[0] = (_gelu(gd_ref[0]) * hseq).astype(BF16)


def rglru(gd, xd, conv_w, conv_b, wa_bd, ba, wx_bd, bx, lam, *, tl):
    b, t, wdt = xd.shape
    kconv = conv_w.shape[0]
    row = pl.BlockSpec((1, tl, wdt), lambda i, j: (i, j, 0))
    cvec = _const_spec((1, wdt))
    return pl.pallas_call(
        functools.partial(_lru_kernel, tl=tl, kconv=kconv),
        grid=(b, t // tl),
        in_specs=[row, row, _const_spec(conv_w.shape), cvec, _const_spec(wa_bd.shape), cvec,
                  _const_spec(wx_bd.shape), cvec, cvec],
        out_specs=row,
        out_shape=jax.ShapeDtypeStruct((b, t, wdt), BF16),
        scratch_shapes=[pltpu.VMEM((8, wdt), F32), pltpu.VMEM((8, wdt), F32)],
        compiler_params=_cp(2),
    )(gd, xd, conv_w, conv_b, wa_bd, ba, wx_bd, bx, lam)


def _block_diag(w):
    n, a, b = w.shape
    eye = jnp.eye(n, dtype=w.dtype)
    return (eye[:, None, :, None] * w[:, :, None, :]).reshape(n * a, n * b)


def _even_weights(w_in, mix_half):
    d = w_in.shape[0]
    nh = mix_half // HEAD_DIM
    hk = nh // NSA_GROUP
    nkv = hk * HEAD_DIM
    o = 0
    wq = w_in[:, o:o + mix_half].reshape(d, nh, HEAD_DIM); o += mix_half
    kvs = []
    for _ in range(6):
        kvs.append(w_in[:, o:o + nkv]); o += nkv
    wg = w_in[:, o:o + 3 * nh].reshape(d, hk, 3 * NSA_GROUP); o += 3 * nh
    wuv = w_in[:, o:]
    kv_of = jnp.arange(nh) // NSA_GROUP
    place = jax.nn.one_hot(kv_of, LANES // HEAD_DIM, dtype=w_in.dtype)
    wq_pad = (wq[:, :, None, :] * place[None, :, :, None]).reshape(d, nh * LANES)
    wg_pad = jnp.pad(wg, ((0, 0), (0, 0), (0, LANES - 3 * NSA_GROUP))).reshape(d, hk * LANES)
    return jnp.concatenate([wq_pad] + kvs + [wg_pad, wuv], axis=1).astype(BF16)


def _compress_weights(pe, w1, w2, hk):
    hid = w1.shape[1]
    pe2 = jnp.tile(pe, (1, hk))
    w1r = w1.reshape(CMP_BLOCK, HEAD_DIM, hid)
    eye = jnp.eye(hk, dtype=w1.dtype)
    w1_bd = (eye[None, :, None, :, None] * w1r[:, None, :, None, :]).reshape(CMP_BLOCK, hk * HEAD_DIM, hk * hid)
    w2_bd = (eye[:, None, :, None] * w2[None, :, None, :]).reshape(hk * hid, hk * HEAD_DIM)
    return pe2, w1_bd.astype(BF16), w2_bd.astype(BF16)


def _ffn_weights(w_up, conv_w, conv_b, w_down):
    return w_up.astype(BF16), conv_w, conv_b.reshape(1, -1), w_down.astype(BF16)


def kernel(x, c, positions, ada_w, ada_b, norm1_g, norm2_g, ev_w_in, ev_w_out, nsa_pe_k, nsa_w1_k, nsa_w2_k, nsa_pe_v, nsa_w1_v, nsa_w2_v, gmlp_ln_g, gmlp_ln_b, gmlp_ws, gmlp_bs, od_w_in, od_w_out, rwkv_mu, rwkv_w0, rwkv_wB, rwkv_a0, rwkv_aB, rwkv_gB, rwkv_xi, rwkv_alpha, rwkv_rho, rwkv_ln_g, rwkv_ln_b, lru_conv_w, lru_conv_b, lru_wa, lru_ba, lru_wx, lru_bx, lru_lambda, ffn_up, ffn_conv_w, ffn_conv_b, ffn_down, final_g):
    b, t, d = x.shape
    depth = ada_w.shape[0]
    mix_half = d // 2
    hk = mix_half // HEAD_DIM // NSA_GROUP
    assert hk * HEAD_DIM == LANES and t % 512 == 0
    tm = 512

    mod = adaln_mod(c, ada_w, ada_b)
    mods = mod.reshape(depth, b, 6, 1, d)
    tabs = rope_tables(positions, tm)
    n_cmp = (t - CMP_BLOCK) // CMP_STRIDE + 1
    ngrp = t // CMP_STRIDE
    pos_end = jnp.pad(positions[:, CMP_BLOCK - 1::CMP_STRIDE][:, :n_cmp], ((0, 0), (0, ngrp - n_cmp)))
    end_tabs = rope_tables(pos_end, ngrp)

    for layer in range(depth):
        i = layer // 2
        sh1, sc1, g1, sh2, sc2, g2 = (mods[layer, :, j] for j in range(6))
        n1 = norm1_g[layer].reshape(1, d)
        n2 = norm2_g[layer].reshape(1, d)
        if layer % 2 == 0:
            w_in = _even_weights(ev_w_in[i], mix_half)
            q, kc, vc, ks, vst, kw, vwt, gates, uv = even_proj(x, n1, sh1, sc1, w_in, tabs, tm=tm, tk_sel=512,
                                                             tk_win=256)
            pek, w1k, w2k = _compress_weights(nsa_pe_k[i], nsa_w1_k[i], nsa_w2_k[i], hk)
            pev, w1v, w2v = _compress_weights(nsa_pe_v[i], nsa_w1_v[i], nsa_w2_v[i], hk)
            kcmp, vcmp = nsa_compress(kc, vc, pek, w1k, w2k, pev, w1v, w2v, end_tabs)
            oc, sel_t = nsa_cmp_topk(q, kcmp, vcmp, tq=256)
            ya = nsa_sel_win(q, ks, vst, kw, vwt, sel_t, oc, gates, tq=256)
            yb = gmlp_gating(uv, gmlp_ln_g[i], gmlp_ln_b[i], gmlp_ws[i], gmlp_bs[i], tg=tm)
            wo = ev_w_out[i].astype(BF16)
        else:
            w_in = od_w_in[i].astype(BF16)
            lora = rwkv_wB.shape[1]
            wb = jnp.pad(rwkv_wB[i], ((0, LANES - lora), (0, 0))).astype(BF16)
            ab = jnp.pad(rwkv_aB[i], ((LANES - rwkv_aB.shape[1], 0), (0, 0))).astype(BF16)
            v1 = lambda p: p.reshape(1, -1)
            (rt, kt, bt, ktt, v, ktp, bp, pc, gout, bonus, gd, xd) = odd_proj(
                x, n1, sh1, sc1, w_in, v1(rwkv_mu[i]), v1(rwkv_w0[i]), wb, v1(rwkv_a0[i]), ab,
                rwkv_gB[i].astype(BF16), v1(rwkv_xi[i]), v1(rwkv_alpha[i]), v1(rwkv_rho[i]), tm=tm)
            ya = rwkv_scan(rt, kt, bt, ktt, v, ktp, bp, pc, gout, bonus, v1(rwkv_ln_g[i]), v1(rwkv_ln_b[i]))
            yb = rglru(gd, xd, lru_conv_w[i], v1(lru_conv_b[i]), _block_diag(lru_wa[i]).astype(BF16),
                       v1(lru_ba[i]), _block_diag(lru_wx[i]).astype(BF16), v1(lru_bx[i]), v1(lru_lambda[i]), tl=tm)
            wo = od_w_out[i].astype(BF16)
        wup, cw, cb, wdn = _ffn_weights(ffn_up[layer], ffn_conv_w[layer], ffn_conv_b[layer], ffn_down[layer])
        x = outproj_ffn(ya, yb, x, wo, g1, n2, sh2, sc2, g2, wup, cw, cb, wdn, final_g.reshape(1, d),
                        tm=tm, cwid=256, last_layer=layer == depth - 1)
    return x
```

```python
import functools
import math

import numpy as np
import jax
import jax.numpy as jnp
from jax import lax
from jax.experimental import pallas as pl
from jax.experimental.pallas import tpu as pltpu

F32 = jnp.float32
BF16 = jnp.bfloat16

HEAD_DIM = 64
ROPE_DIM = HEAD_DIM // 4
ROPE_THETA = 500000.0
EPS = 1e-6
NEG_INF = -1e30
LANES = 128

NSA_GROUP = 4
CMP_BLOCK = 32
CMP_STRIDE = 16
SLC_BLOCK = 64
SLC_SHIFT = 6
SLC_TOPK = 16
WINDOW = 512
FORCE_SCORE = 1e3
MASK_BIG = -NEG_INF
LOG2E = math.log2(math.e)
GMLP_CHUNK = 128
RWKV_CHUNK = 64
LRU_C = 8.0

VMEM_LIMIT = 56 * 1024 * 1024


def _cp(n_axes, vmem=VMEM_LIMIT):
    return pltpu.CompilerParams(dimension_semantics=("arbitrary",) * n_axes, vmem_limit_bytes=vmem)


def _const_spec(shape):
    nd = len(shape)
    return pl.BlockSpec(shape, lambda *_: (0,) * nd, pipeline_mode=pl.Buffered(1))


def _dot(a, b):
    return jnp.dot(a, b, preferred_element_type=F32)


def _dot_nt(a, b):
    return lax.dot_general(a, b, (((1,), (1,)), ((), ())), preferred_element_type=F32)


def _dot_tn(a, b):
    return lax.dot_general(a, b, (((0,), (0,)), ((), ())), preferred_element_type=F32)


def _split2(z):
    hi = z.astype(BF16)
    lo = (z - hi.astype(F32)).astype(BF16)
    return hi, lo


def _split3(z):
    hi = z.astype(BF16)
    r = z - hi.astype(F32)
    mid = r.astype(BF16)
    lo = (r - mid.astype(F32)).astype(BF16)
    return hi, mid, lo


def _dot_exact_rhs(z, m):
    hi, mid, lo = _split3(z)
    return _dot(hi, m) + _dot(mid, m) + _dot(lo, m)


def _lhs_exact_dot(m, z):
    hi, mid, lo = _split3(z)
    return _dot(m, hi) + _dot(m, mid) + _dot(m, lo)


def _norm_mod(x, g, shift, scale):
    y = x * lax.rsqrt(jnp.mean(x * x, axis=-1, keepdims=True) + EPS)
    return (y * g) * (1.0 + scale) + shift


def _gelu(x):
    return jax.nn.gelu(x, approximate=True)


def _sigmoid(x):
    return jax.nn.sigmoid(x)


def _iota(shape, axis):
    return lax.broadcasted_iota(jnp.int32, shape, axis)


def _adaln_kernel(c_ref, w_ref, b_ref, o_ref):
    cond = c_ref[...]
    cond = cond * _sigmoid(cond)
    o_ref[0] = _dot(cond.astype(BF16), w_ref[0].astype(BF16)) + b_ref[0]


def adaln_mod(c, ada_w, ada_b):
    depth, d, n = ada_w.shape
    b = c.shape[0]
    rows = 16
    cp = jnp.zeros((rows, d), F32).at[:b].set(c)
    tn = 1536
    out = pl.pallas_call(
        _adaln_kernel,
        grid=(depth, n // tn),
        in_specs=[pl.BlockSpec((rows, d), lambda l, j: (0, 0)),
                  pl.BlockSpec((1, d, tn), lambda l, j: (l, 0, j)),
                  pl.BlockSpec((1, 1, tn), lambda l, j: (l, 0, j))],
        out_specs=pl.BlockSpec((1, rows, tn), lambda l, j: (l, 0, j)),
        out_shape=jax.ShapeDtypeStruct((depth, rows, n), F32),
        compiler_params=_cp(2),
    )(cp, ada_w, ada_b.reshape(depth, 1, n))
    return out[:, :b]


def _rope_kernel(pos_ref, inv_ref, c_ref, sa_ref, sb_ref):
    ang = pos_ref[0].astype(F32) * inv_ref[...]
    cos, sin = jnp.cos(ang), jnp.sin(ang)
    d = _iota(ang.shape, 1) % HEAD_DIM
    c_ref[0] = jnp.where(d < ROPE_DIM, cos, 1.0)
    sa_ref[0] = jnp.where(d < ROPE_DIM // 2, -sin, 0.0)
    sb_ref[0] = jnp.where((d >= ROPE_DIM // 2) & (d < ROPE_DIM), sin, 0.0)


def rope_tables(pos, tm):
    b, n = pos.shape
    half = ROPE_DIM // 2
    inv = ROPE_THETA ** (-jnp.arange(0, ROPE_DIM, 2, dtype=F32) / ROPE_DIM)
    inv_lane = jnp.tile(inv, LANES // half).reshape(1, LANES)
    spec = pl.BlockSpec((1, tm, LANES), lambda i, j: (i, j, 0))
    return pl.pallas_call(
        _rope_kernel,
        grid=(b, n // tm),
        in_specs=[pl.BlockSpec((1, tm, 1), lambda i, j: (i, j, 0)),
                  pl.BlockSpec((1, LANES), lambda i, j: (0, 0))],
        out_specs=[spec, spec, spec],
        out_shape=[jax.ShapeDtypeStruct((b, n, LANES), F32)] * 3,
        compiler_params=_cp(2),
    )(pos[..., None], inv_lane)


def _rope(y, c, sa, sb):
    return y * c + pltpu.roll(y, LANES - ROPE_DIM // 2, 1) * sa + pltpu.roll(y, ROPE_DIM // 2, 1) * sb


def _even_proj_kernel(x_ref, g_ref, sh_ref, sc_ref, w_ref, c_ref, sa_ref, sb_ref,
                      q_ref, kc_ref, vc_ref, ks_ref, vst_ref, kw_ref, vwt_ref, gate_ref, uv_ref,
                      *, nq, nuv):
    h = _norm_mod(x_ref[0], g_ref[...], sh_ref[0], sc_ref[0]).astype(BF16)
    c, sa, sb = c_ref[0], sa_ref[0], sb_ref[0]
    scale = HEAD_DIM ** -0.5 * LOG2E
    yq = _dot(h, w_ref[:, 0:nq])
    for j in range(nq // LANES):
        sl = slice(j * LANES, (j + 1) * LANES)
        q_ref[0, :, sl] = (_rope(yq[:, sl], c, sa, sb) * scale).astype(BF16)
    o = nq
    ykv = _dot(h, w_ref[:, o:o + 6 * LANES])
    kc_ref[0] = ykv[:, 0:LANES]
    vc_ref[0] = ykv[:, LANES:2 * LANES]
    tm = ykv.shape[0]
    blk = jnp.right_shift(pl.program_id(1) * tm + _iota((tm, LANES), 0), SLC_SHIFT)
    ks_ref[0, :, 0:LANES] = _rope(ykv[:, 2 * LANES:3 * LANES], c, sa, sb).astype(BF16)
    ks_ref[0, :, LANES:2 * LANES] = jnp.where(_iota((tm, LANES), 1) == blk, MASK_BIG, 0.0).astype(BF16)
    kw_ref[0] = _rope(ykv[:, 4 * LANES:5 * LANES], c, sa, sb).astype(BF16)
    for ref, lo in ((vst_ref, 3 * LANES), (vwt_ref, 5 * LANES)):
        tile = ref.shape[3]
        for i in range(ref.shape[1]):
            ref[0, i] = ykv[i * tile:(i + 1) * tile, lo:lo + LANES].T.astype(BF16)
    o += 6 * LANES
    gate_ref[0] = _sigmoid(_dot(h, w_ref[:, o:o + 2 * LANES]))
    o += 2 * LANES
    uv_ref[0] = _dot(h, w_ref[:, o:o + nuv])


def even_proj(x, g, shift, scale, w, tabs, *, tm, tk_sel, tk_win):
    b, t, d = x.shape
    n = w.shape[1]
    nq = 2 * NSA_GROUP * LANES
    nuv = n - nq - 8 * LANES
    row = lambda wdt: pl.BlockSpec((1, tm, wdt), lambda i, j: (i, j, 0))
    vec = pl.BlockSpec((1, 1, d), lambda i, j: (i, 0, 0))
    rows = lambda wdt, dt: (row(wdt), jax.ShapeDtypeStruct((b, t, wdt), dt))
    vt = lambda tk: (pl.BlockSpec((1, tm // tk, LANES, tk), lambda i, j: (i, j, 0, 0)),
                     jax.ShapeDtypeStruct((b, t // tk, LANES, tk), BF16))
    outs = [rows(nq, BF16), rows(LANES, F32), rows(LANES, F32), rows(2 * LANES, BF16), vt(tk_sel),
            rows(LANES, BF16), vt(tk_win), rows(2 * LANES, F32), rows(nuv, F32)]
    return pl.pallas_call(
        functools.partial(_even_proj_kernel, nq=nq, nuv=nuv),
        grid=(b, t // tm),
        in_specs=[row(d), _const_spec((1, d)), vec, vec, _const_spec((d, n)),
                  row(LANES), row(LANES), row(LANES)],
        out_specs=[s for s, _ in outs],
        out_shape=[s for _, s in outs],
        compiler_params=_cp(2),
    )(x, g, shift, scale, w, *tabs)


def _compress_kernel(kc_ref, vc_ref, pek_ref, w1k_ref, w2k_ref, pev_ref, w1v_ref, w2v_ref,
                     c_ref, sa_ref, sb_ref, ko_ref, vo_ref, *, ngrp, n_cmp):
    half = CMP_BLOCK // 2

    def phi(src_ref, pe_ref, w1_ref, w2_ref):
        p = jnp.zeros((ngrp, w1_ref.shape[2]), F32)
        q = jnp.zeros((ngrp, w1_ref.shape[2]), F32)
        for l in range(half):
            a = src_ref[0, pl.ds(l, ngrp, stride=CMP_STRIDE), :]
            p = p + _dot((a + pe_ref[l:l + 1, :]).astype(BF16), w1_ref[l])
            q = q + _dot((a + pe_ref[half + l:half + l + 1, :]).astype(BF16), w1_ref[half + l])
        hid = p + pltpu.roll(q, ngrp - 1, 0)
        return _dot(_gelu(hid).astype(BF16), w2_ref[...])

    row = _iota((ngrp, LANES), 0)
    kc = _rope(phi(kc_ref, pek_ref, w1k_ref, w2k_ref), c_ref[0], sa_ref[0], sb_ref[0])
    ko_ref[0] = jnp.where(row < n_cmp, kc, 0.0).astype(BF16)
    vc = phi(vc_ref, pev_ref, w1v_ref, w2v_ref)
    vo_ref[0] = jnp.where(row < n_cmp, vc, 0.0).T.astype(BF16)


def nsa_compress(kc, vc, pek, w1k, w2k, pev, w1v, w2v, end_tabs):
    b, t, _ = kc.shape
    ngrp = t // CMP_STRIDE
    n_cmp = (t - CMP_BLOCK) // CMP_STRIDE + 1
    tok = pl.BlockSpec((1, t, LANES), lambda i: (i, 0, 0))
    grp = pl.BlockSpec((1, ngrp, LANES), lambda i: (i, 0, 0))
    return pl.pallas_call(
        functools.partial(_compress_kernel, ngrp=ngrp, n_cmp=n_cmp),
        grid=(b,),
        in_specs=[tok, tok, _const_spec(pek.shape), _const_spec(w1k.shape), _const_spec(w2k.shape),
                  _const_spec(pev.shape), _const_spec(w1v.shape), _const_spec(w2v.shape), grp, grp, grp],
        out_specs=[grp, pl.BlockSpec((1, LANES, ngrp), lambda i: (i, 0, 0))],
        out_shape=[jax.ShapeDtypeStruct((b, ngrp, LANES), BF16), jax.ShapeDtypeStruct((b, LANES, ngrp), BF16)],
        compiler_params=_cp(1),
    )(kc, vc, pek, w1k, w2k, pev, w1v, w2v, *end_tabs)


def _cmp_topk_kernel(q_ref, kc_ref, vc_ref, ovt_ref, oc_ref, sel_ref, *, tq, n_cmp, ncp, n_slc, k_top):
    t0 = pl.program_id(2) * tq
    kc, vct = kc_ref[0], vc_ref[0]
    end = CMP_BLOCK - 1
    n_row, t_col = _iota((ncp, tq), 0), t0 + _iota((ncp, tq), 1)
    valid_t = (n_row * CMP_STRIDE + end <= t_col) & (n_row < n_cmp)
    psum_t = jnp.zeros((ncp, tq), F32)
    for g in range(NSA_GROUP):
        qg = q_ref[0, :, g * LANES:(g + 1) * LANES]
        st = jnp.where(valid_t, _dot_nt(kc, qg), NEG_INF)
        et = jnp.where(valid_t, jnp.exp2(st - jnp.max(st, axis=0, keepdims=True)), 0.0)
        lt = jnp.sum(et, axis=0, keepdims=True)
        pt = et * (1.0 / jnp.where(lt > 0.0, lt, 1.0))
        oc_ref[0, g * LANES:(g + 1) * LANES, :] = _dot(vct, pt.astype(BF16))
        psum_t = psum_t + pt
    hi, lo = _split2(psum_t)
    imp = _dot(ovt_ref[...], hi) + _dot(ovt_ref[...], lo)
    j = _iota((n_slc, tq), 0)
    t = t0 + _iota((n_slc, tq), 1)
    cur = jnp.right_shift(t, SLC_SHIFT)
    forced = (j == 0) | (j == cur) | (j == cur - 1)
    score = jnp.where(forced, FORCE_SCORE, jnp.where(j * SLC_BLOCK <= t, imp, -1.0))
    rank = jnp.zeros((n_slc, tq), jnp.int32)
    for jj in range(n_slc):
        r = score[jj:jj + 1, :]
        beats = (r > score) | ((r == score) & (j > jj))
        rank = rank + beats.astype(jnp.int32)
    sel_ref[0, 0] = jnp.where(rank < k_top, 1.0, 0.0).astype(BF16)


def nsa_cmp_topk(q, kcmp, vcmp, *, tq):
    b, t, nq = q.shape
    hk = nq // (NSA_GROUP * LANES)
    ncp = kcmp.shape[1]
    n_cmp = (t - CMP_BLOCK) // CMP_STRIDE + 1
    n_slc = t // SLC_BLOCK
    cs = np.arange(ncp) * CMP_STRIDE
    ss = np.arange(n_slc) * SLC_BLOCK
    ov = np.clip(np.minimum(cs[:, None] + CMP_BLOCK, ss[None, :] + SLC_BLOCK)
                 - np.maximum(cs[:, None], ss[None, :]), 0, None).astype(np.float32) / CMP_BLOCK
    ov[n_cmp:] = 0.0
    ovt = jnp.asarray(ov.T, BF16)
    gq = NSA_GROUP * LANES
    return pl.pallas_call(
        functools.partial(_cmp_topk_kernel, tq=tq, n_cmp=n_cmp, ncp=ncp, n_slc=n_slc,
                          k_top=min(SLC_TOPK, n_slc)),
        grid=(b, hk, t // tq),
        in_specs=[pl.BlockSpec((1, tq, gq), lambda i, k, j: (i, j, k)),
                  pl.BlockSpec((1, ncp, LANES), lambda i, k, j: (i, 0, 0)),
                  pl.BlockSpec((1, LANES, ncp), lambda i, k, j: (i, 0, 0)),
                  _const_spec((n_slc, ncp))],
        out_specs=[pl.BlockSpec((1, gq, tq), lambda i, k, j: (i, k, j)),
                   pl.BlockSpec((1, 1, n_slc, tq), lambda i, k, j: (i, k, 0, j))],
        out_shape=[jax.ShapeDtypeStruct((b, nq, t), F32),
                   jax.ShapeDtypeStruct((b, hk, n_slc, t), BF16)],
        compiler_params=_cp(3),
    )(q, kcmp, vcmp, ovt)


def _softmax_rows(s_ref, p_ref, g, m_old, rb):
    rows, cols = s_ref.shape[1], s_ref.shape[2]
    fold = lambda z, op: op(z.reshape(rb // 8, 8, cols), axis=0)
    mx = fold(s_ref[g, 0:rb, :], jnp.max)
    for r in range(rb, rows, rb):
        mx = jnp.maximum(mx, fold(s_ref[g, r:r + rb, :], jnp.max))
    m_new = jnp.max(mx, axis=0, keepdims=True)
    if m_old is not None:
        m_new = jnp.maximum(m_old, m_new)
    acc = None
    for r in range(0, rows, rb):
        p = jnp.exp2(s_ref[g, r:r + rb, :] - m_new)
        part = fold(p, jnp.sum)
        acc = part if acc is None else acc + part
        p_ref[g, r:r + rb, :] = p.astype(BF16)
    return m_new, jnp.sum(acc, axis=0, keepdims=True)


def _sel_win_kernel(q_ref, ks_ref, vst_ref, kw_ref, vwt_ref, sel_ref, oc_ref, gate_ref, y_ref,
                    s_ref, p_ref, m_ref, l_ref, acc_ref, sw_ref, pw_ref, *, tq, tk, n_slc):
    kv = pl.program_id(1)
    qi = pl.program_id(2)
    q0 = qi * tq
    grp = NSA_GROUP
    rb = 64
    qt = [q_ref[0, :, g * LANES:(g + 1) * LANES].astype(F32).T.astype(BF16) for g in range(grp)]
    unsel = (sel_ref[0, 0].astype(F32) - 1.0).astype(BF16)
    unsel = jnp.concatenate([unsel, jnp.zeros((LANES - n_slc, tq), BF16)], axis=0)
    qa = [jnp.concatenate([z, unsel], axis=0) for z in qt]

    m_ref[...] = jnp.full(m_ref.shape, NEG_INF, F32)
    l_ref[...] = jnp.zeros(l_ref.shape, F32)
    acc_ref[...] = jnp.zeros(acc_ref.shape, F32)

    def score_head(kt, slot, g, diagonal=False):
        k0 = pl.multiple_of(kt * tk, tk)
        s = _dot(ks_ref[0, pl.ds(k0, tk), :], qa[g])
        if diagonal:
            causal = k0 + _iota((tk, tq), 0) <= q0 + _iota((tk, tq), 1)
            s = s + jnp.where(causal, 0.0, NEG_INF)
        s_ref[slot * grp + g] = s

    def update_head(kt, slot, g):
        m_old = m_ref[g]
        m_new, l_add = _softmax_rows(s_ref, p_ref, slot * grp + g, m_old, rb)
        alpha = jnp.exp2(m_old - m_new)
        m_ref[g] = m_new
        l_ref[g] = alpha * l_ref[g] + l_add
        acc_ref[g] = alpha * acc_ref[g] + _dot(vst_ref[0, kt], p_ref[slot * grp + g])

    def score_tile(kt, slot, diagonal=False):
        for g in range(grp):
            score_head(kt, slot, g, diagonal)

    def update_tile(kt, slot):
        for g in range(grp):
            update_head(kt, slot, g)

    def score_and_update(kt_next, kt_cur, slot_cur):
        for g in range(grp):
            score_head(kt_next, 1 - slot_cur, g)
            update_head(kt_cur, slot_cur, g)

    kt_diag = q0 // tk
    pairs = kt_diag // 2
    odd = kt_diag % 2 == 1
    score_tile(kt_diag, 0, diagonal=True)

    def sel_body(j, carry):
        score_and_update(2 * j, jnp.where(j == 0, kt_diag, 2 * j - 1), 0)
        score_and_update(2 * j + 1, 2 * j, 1)
        return carry

    lax.fori_loop(0, pairs, sel_body, 0)

    @pl.when(odd)
    def _():
        score_tile(kt_diag - 1, 1)

    nwt = WINDOW // tq + 1
    kt0 = jnp.maximum(qi - WINDOW // tq, 0)
    k0 = pl.multiple_of(kt0 * tq, tq)
    kp, tt = k0 + _iota((nwt * tq, tq), 0), q0 + _iota((nwt * tq, tq), 1)
    bias_w = jnp.where((kp <= tt) & (kp > tt - WINDOW), 0.0, NEG_INF)
    kt_last = jnp.where(pairs == 0, kt_diag, 2 * pairs - 1)
    for g in range(grp):
        sw_ref[g] = _dot(kw_ref[0, pl.ds(k0, nwt * tq), :], qt[g]) + bias_w
        update_head(kt_last, 0, g)

    @pl.when(odd)
    def _():
        update_tile(kt_diag - 1, 1)

    o_w = []
    for g in range(grp):
        _, l_w = _softmax_rows(sw_ref, pw_ref, g, None, rb)
        acc_w = _dot(vwt_ref[0, kt0], pw_ref[g, 0:tq, :])
        for i in range(1, nwt):
            acc_w = acc_w + _dot(vwt_ref[0, kt0 + i], pw_ref[g, i * tq:(i + 1) * tq, :])
        o_w.append(acc_w / l_w)

    gate_t = gate_ref[0].T
    ys = []
    for g in range(NSA_GROUP):
        y_t = (gate_t[3 * g:3 * g + 1, :] * oc_ref[0, g * LANES:(g + 1) * LANES, :]
               + gate_t[3 * g + 1:3 * g + 2, :] * (acc_ref[g] / l_ref[g])
               + gate_t[3 * g + 2:3 * g + 3, :] * o_w[g])
        ys.append(y_t.T)
    low = _iota((tq, LANES), 1) < HEAD_DIM
    first_kv = kv == 0
    for j in range(NSA_GROUP // 2):
        a, b2 = ys[2 * j], ys[2 * j + 1]
        lo_half = jnp.where(first_kv, a, pltpu.roll(a, HEAD_DIM, 1))
        hi_half = jnp.where(first_kv, pltpu.roll(b2, HEAD_DIM, 1), b2)
        y_ref[0, :, j * LANES:(j + 1) * LANES] = jnp.where(low, lo_half, hi_half).astype(BF16)


def nsa_sel_win(q, ks, vst, kw, vwt, sel_t, oc, gates, *, tq):
    b, t, nq = q.shape
    hk = nq // (NSA_GROUP * LANES)
    n_slc = t // SLC_BLOCK
    tk = vst.shape[3]
    assert vwt.shape[3] == tq and tk % tq == 0
    gq = NSA_GROUP * LANES
    full = pl.BlockSpec((1, t, LANES), lambda i, k, j: (i, 0, 0))
    tiles = lambda a: pl.BlockSpec((1,) + a.shape[1:], lambda i, k, j: (i, 0, 0, 0))
    qspec = pl.BlockSpec((1, tq, gq), lambda i, k, j: (i, j, k))
    return pl.pallas_call(
        functools.partial(_sel_win_kernel, tq=tq, tk=tk, n_slc=n_slc),
        grid=(b, hk, t // tq),
        in_specs=[qspec, pl.BlockSpec((1, t, 2 * LANES), lambda i, k, j: (i, 0, 0)), tiles(vst), full, tiles(vwt),
                  pl.BlockSpec((1, 1, n_slc, tq), lambda i, k, j: (i, k, 0, j)),
                  pl.BlockSpec((1, gq, tq), lambda i, k, j: (i, k, j)),
                  pl.BlockSpec((1, tq, LANES), lambda i, k, j: (i, j, k))],
        out_specs=pl.BlockSpec((1, tq, NSA_GROUP * HEAD_DIM), lambda i, k, j: (i, j, k)),
        out_shape=jax.ShapeDtypeStruct((b, t, hk * NSA_GROUP * HEAD_DIM), BF16),
        scratch_shapes=[pltpu.VMEM((2 * NSA_GROUP, tk, tq), F32), pltpu.VMEM((2 * NSA_GROUP, tk, tq), BF16),
                        pltpu.VMEM((NSA_GROUP, 1, tq), F32), pltpu.VMEM((NSA_GROUP, 1, tq), F32),
                        pltpu.VMEM((NSA_GROUP, LANES, tq), F32),
                        pltpu.VMEM((NSA_GROUP, WINDOW + tq, tq), F32),
                        pltpu.VMEM((NSA_GROUP, WINDOW + tq, tq), BF16)],
        compiler_params=_cp(3),
    )(q, ks, vst, kw, vwt, sel_t, oc, gates)


def _gmlp_kernel(u_ref, v_ref, g_ref, b_ref, avg_ref, ws_ref, bs_ref, y_ref, *, tg):
    u = _gelu(u_ref[0])
    v = _gelu(v_ref[0])
    avg = avg_ref[...]

    def gmean(z):
        hi, lo = _split2(z)
        return _dot(hi, avg) + _dot(lo, avg)

    d = v - gmean(v)
    vn = (d * lax.rsqrt(gmean(d * d) + EPS) * g_ref[...] + b_ref[...]).astype(BF16)
    c = GMLP_CHUNK
    causal = _iota((c, c), 0) >= _iota((c, c), 1)
    low = _iota((c, LANES), 1) < HEAD_DIM
    width = u.shape[1]
    for j in range(width // LANES):
        w0 = jnp.where(causal, ws_ref[2 * j], 0.0).astype(BF16)
        w1 = jnp.where(causal, ws_ref[2 * j + 1], 0.0).astype(BF16)
        cols = slice(j * LANES, (j + 1) * LANES)
        for ci in range(tg // c):
            rows = slice(ci * c, (ci + 1) * c)
            v2 = vn[rows, cols]
            mixed = jnp.where(low, _dot(w0, v2), _dot(w1, v2)) + bs_ref[:, cols]
            y_ref[0, rows, cols] = (u[rows, cols] * mixed).astype(BF16)


def gmlp_gating(uv, ln_g, ln_b, ws, bs, *, tg):
    b, t, w2 = uv.shape
    w = w2 // 2
    ngrp = w // HEAD_DIM
    avg = jnp.asarray(np.kron(np.eye(ngrp), np.full((HEAD_DIM, HEAD_DIM), 1.0 / HEAD_DIM)), BF16)
    bs_exp = jnp.repeat(bs.T, HEAD_DIM, axis=1)
    return pl.pallas_call(
        functools.partial(_gmlp_kernel, tg=tg),
        grid=(b, t // tg),
        in_specs=[pl.BlockSpec((1, tg, w), lambda i, j: (i, j, 0)),
                  pl.BlockSpec((1, tg, w), lambda i, j: (i, j, 1)),
                  _const_spec((1, w)), _const_spec((1, w)), _const_spec((w, w)),
                  _const_spec(ws.shape), _const_spec((GMLP_CHUNK, w))],
        out_specs=pl.BlockSpec((1, tg, w), lambda i, j: (i, j, 0)),
        out_shape=jax.ShapeDtypeStruct((b, t, w), BF16),
        compiler_params=_cp(2),
    )(uv, uv, ln_g.reshape(1, w), ln_b.reshape(1, w), avg, ws, bs_exp)


def _outproj_ffn_kernel(ya_ref, yb_ref, x_ref, wo_ref, g1_ref, n2_ref, sh_ref, sc_ref, g2_ref,
                        wup_ref, cw_ref, cb_ref, wdn_ref, fin_ref, o_ref, carry_ref, h_ref, up_ref, act_ref,
                        *, tm, nch, cwid, rs, last_layer):
    @pl.when(pl.program_id(1) == 0)
    def _():
        carry_ref[...] = jnp.zeros(carry_ref.shape, F32)

    half = ya_ref.shape[2]
    y = _dot(ya_ref[0], wo_ref[0:half, :]) + _dot(yb_ref[0], wo_ref[half:2 * half, :])
    x1 = x_ref[0] + g1_ref[0] * y
    o_ref[0] = x1
    h_ref[...] = _norm_mod(x1, n2_ref[...], sh_ref[0], sc_ref[0]).astype(BF16)
    kconv = cw_ref.shape[0]
    cols = lambda idx: slice(idx * cwid, (idx + 1) * cwid)

    def conv_rows(slot, half, idx, r0):
        w = cw_ref[:, cols(idx)]
        out = cb_ref[:, cols(idx)]
        for d in range(kconv):
            out = out + w[kconv - 1 - d:kconv - d, :] * up_ref[slot, half, r0 + 8 - d:r0 + 8 - d + rs, :]
        return out

    for c in range(nch):
        slot = c % 2
        for half, idx in ((0, c), (1, nch + c)):
            up_ref[slot, half, 0:8, :] = carry_ref[idx]
            up_ref[slot, half, 8:tm + 8, :] = _dot(h_ref[...], wup_ref[:, cols(idx)])
            carry_ref[idx] = up_ref[slot, half, tm:tm + 8, :]
        for r0 in range(0, tm, rs):
            a = conv_rows(slot, 0, c, r0)
            act = a * _sigmoid(a) * conv_rows(slot, 1, nch + c, r0)
            act_ref[r0:r0 + rs, cols(c)] = act.astype(BF16)
    x2 = o_ref[0] + g2_ref[0] * _dot(act_ref[...], wdn_ref[...])
    if last_layer:
        x2 = x2 * lax.rsqrt(jnp.mean(x2 * x2, axis=-1, keepdims=True) + EPS) * fin_ref[...]
    o_ref[0] = x2


def outproj_ffn(ya, yb, x, wo, g1, n2g, sh2, sc2, g2, wup, cw, cb, wdn, final_g, *, tm, cwid, last_layer):
    b, t, d = x.shape
    half = ya.shape[2]
    nch = wdn.shape[0] // cwid
    assert nch * cwid == wdn.shape[0] and wup.shape[1] == 2 * nch * cwid
    row = lambda wdt: pl.BlockSpec((1, tm, wdt), lambda i, j: (i, j, 0))
    vec = pl.BlockSpec((1, 1, d), lambda i, j: (i, 0, 0))
    return pl.pallas_call(
        functools.partial(_outproj_ffn_kernel, tm=tm, nch=nch, cwid=cwid, rs=64, last_layer=last_layer),
        grid=(b, t // tm),
        in_specs=[row(half), row(half), row(d), _const_spec(wo.shape), vec, _const_spec((1, d)), vec, vec, vec,
                  _const_spec(wup.shape), _const_spec(cw.shape), _const_spec(cb.shape), _const_spec(wdn.shape),
                  _const_spec((1, d))],
        out_specs=row(d),
        out_shape=jax.ShapeDtypeStruct((b, t, d), F32),
        scratch_shapes=[pltpu.VMEM((2 * nch, 8, cwid), F32), pltpu.VMEM((tm, d), BF16),
                        pltpu.VMEM((2, 2, tm + 8, cwid), F32), pltpu.VMEM((tm, nch * cwid), BF16)],
        compiler_params=_cp(2),
    )(ya, yb, x, wo, g1, n2g, sh2, sc2, g2, wup, cw, cb, wdn, final_g)


def _odd_proj_kernel(x_ref, g_ref, sh_ref, sc_ref, w_ref, mu_ref, w0_ref, wb_ref, a0_ref, ab_ref, gb_ref,
                     xi_ref, al_ref, rho_ref, ones_ref, tril_ref,
                     rt_ref, kt_ref, bt_ref, ktt_ref, v_ref, ktp_ref, bp_ref, pc_ref, gout_ref, bonus_ref,
                     gd_ref, xd_ref, carry_ref, *, tm, wmix, nc):
    @pl.when(pl.program_id(1) == 0)
    def _():
        carry_ref[...] = jnp.zeros(carry_ref.shape, F32)

    h = _norm_mod(x_ref[0], g_ref[...], sh_ref[0], sc_ref[0]).astype(BF16)
    pc = _dot(h, w_ref[:, 0:nc])
    pd = _dot(h, w_ref[:, nc:nc + 2 * wmix])
    gd_ref[0] = pd[:, 0:wmix]
    xd_ref[0] = pd[:, wmix:2 * wmix]

    row = _iota((tm, nc), 0)
    prev = jnp.where(row == 0, carry_ref[0:1, :], pltpu.roll(pc, 1, 0))
    carry_ref[0:1, :] = pc[tm - 1:tm, :]
    pc = pc + mu_ref[...] * (prev - pc)

    r, k, v = pc[:, 0:wmix], pc[:, wmix:2 * wmix], pc[:, 2 * wmix:3 * wmix]
    wa = pc[:, 3 * wmix:3 * wmix + LANES]
    gl = pc[:, 3 * wmix + LANES:3 * wmix + 2 * LANES]
    log_w = -math.exp(-0.5) * _sigmoid(w0_ref[...] + _dot(jnp.tanh(wa).astype(BF16), wb_ref[...]))
    a = _sigmoid(a0_ref[...] + _dot(wa.astype(BF16), ab_ref[...]))
    gout_ref[0] = _dot(_sigmoid(gl).astype(BF16), gb_ref[...])

    ones = ones_ref[...]

    def gsum(z):
        hi, lo = _split2(z)
        return jnp.concatenate([_dot(hi[:, j:j + LANES], ones) + _dot(lo[:, j:j + LANES], ones)
                                for j in range(0, wmix, LANES)], axis=1)

    kap = k * xi_ref[...]
    kap = kap * lax.rsqrt(gsum(kap * kap) + EPS)
    kt = k * (1.0 + (a - 1.0) * al_ref[...])
    bonus_ref[0] = gsum(r * rho_ref[...] * kt) * v
    bvec = a * kap

    c = RWKV_CHUNK
    cs_chunks = [_lhs_exact_dot(tril_ref[...], log_w[i:i + c]) for i in range(0, tm, c)]
    cs = jnp.concatenate(cs_chunks, axis=0)
    tot = jnp.concatenate([jnp.broadcast_to(z[c - 1:c, :], z.shape) for z in cs_chunks], axis=0)
    dec_out = jnp.exp(-cs)
    dec_end = jnp.exp(tot - cs)
    rt_ref[0] = (r * jnp.exp(cs)).astype(BF16)
    kt_ref[0] = (kap * jnp.exp(cs - log_w)).astype(BF16)
    bt_ref[0] = (bvec * dec_out).astype(BF16)
    ktt_ref[0] = (kt * dec_out).astype(BF16)
    v_ref[0] = v.astype(BF16)
    ktp_ref[0] = (kt * dec_end).astype(BF16)
    bp_ref[0] = (bvec * dec_end).astype(BF16)
    pc_ref[0] = jnp.exp(tot)


def odd_proj(x, g, shift, scale, w, mu, w0, wb, a0, ab, gb, xi, alpha, rho, *, tm):
    b, t, d = x.shape
    wmix = w0.shape[1]
    nc = mu.shape[1]
    ones = jnp.asarray(np.kron(np.eye(LANES // HEAD_DIM), np.ones((HEAD_DIM, HEAD_DIM))), BF16)
    tril = jnp.asarray(np.tril(np.ones((RWKV_CHUNK, RWKV_CHUNK))), BF16)
    row = lambda wdt: pl.BlockSpec((1, tm, wdt), lambda i, j: (i, j, 0))
    vec = pl.BlockSpec((1, 1, d), lambda i, j: (i, 0, 0))
    cvec = _const_spec((1, wmix))
    outs = [BF16] * 7 + [F32] * 5
    return pl.pallas_call(
        functools.partial(_odd_proj_kernel, tm=tm, wmix=wmix, nc=nc),
        grid=(b, t // tm),
        in_specs=[row(d), _const_spec((1, d)), vec, vec, _const_spec(w.shape), _const_spec((1, nc)),
                  cvec, _const_spec(wb.shape), cvec, _const_spec(ab.shape), _const_spec(gb.shape),
                  cvec, cvec, cvec, _const_spec(ones.shape), _const_spec(tril.shape)],
        out_specs=[row(wmix)] * 12,
        out_shape=[jax.ShapeDtypeStruct((b, t, wmix), dt) for dt in outs],
        scratch_shapes=[pltpu.VMEM((8, nc), F32)],
        compiler_params=_cp(2),
    )(x, g, shift, scale, w, mu, w0, wb, a0, ab, gb, xi, alpha, rho, ones, tril)


def _rwkv_scan_kernel(rt_ref, kt_ref, bt_ref, ktt_ref, v_ref, ktp_ref, bp_ref, pc_ref, g_ref, bonus_ref,
                      lng_ref, lnb_ref, y_ref, s_ref, *, nb, nh, group):
    @pl.when(pl.program_id(0) == 0)
    def _():
        s_ref[...] = jnp.zeros(s_ref.shape, F32)

    c = RWKV_CHUNK
    ti, si = _iota((2 * c, c), 0), _iota((2 * c, c), 1)
    tri = ((ti < c) & (ti > si)) | ((ti >= c) & (ti - c >= si))
    eye = jnp.where(_iota((c, c), 0) == _iota((c, c), 1), 1.0, 0.0)
    all_chains = [(b, h) for b in range(nb) for h in range(nh)]
    col = lambda h: slice(h * HEAD_DIM, (h + 1) * HEAD_DIM)
    for c0 in range(0, len(all_chains), group):
        chains = all_chains[c0:c0 + group]
        ld = lambda ref: [ref[b, :, col(h)] for b, h in chains]
        kt, rt, bt, ktt, v = ld(kt_ref), ld(rt_ref), ld(bt_ref), ld(ktt_ref), ld(v_ref)
        kr = [jnp.concatenate([k_, r_], axis=0) for k_, r_ in zip(kt, rt)]
        nq = [jnp.where(tri, _dot_nt(x, y), 0.0) for x, y in zip(kr, bt)]
        aq = [jnp.where(tri, _dot_nt(x, y), 0.0).astype(BF16) for x, y in zip(kr, ktt)]
        aqv = [_dot(x, y) for x, y in zip(aq, v)]
        npow = [x[0:c] for x in nq]
        tinv = [eye - x for x in npow]
        for _ in range(int(math.log2(c)) - 1):
            npb = [x.astype(BF16) for x in npow]
            npow = [_dot(x, x) for x in npb]
            tinv = [t_ + _dot(t_.astype(BF16), p_.astype(BF16)) for t_, p_ in zip(tinv, npow)]
        s = [s_ref[c0 + i] for i in range(len(chains))]
        krs = [_dot_nt(x, s_.astype(BF16)) for x, s_ in zip(kr, s)]
        u = [_dot(t_.astype(BF16), (a_[0:c] + b_[0:c]).astype(BF16)).astype(BF16)
             for t_, a_, b_ in zip(tinv, krs, aqv)]
        y = [a_[c:2 * c] + b_[c:2 * c] - _dot(q_[c:2 * c].astype(BF16), u_)
             for a_, b_, q_, u_ in zip(krs, aqv, nq, u)]
        ktp, bp = ld(ktp_ref), ld(bp_ref)
        for i, (b, h) in enumerate(chains):
            upd = _dot_tn(jnp.concatenate([v[i], -u[i]], axis=0), jnp.concatenate([ktp[i], bp[i]], axis=0))
            s_ref[c0 + i] = s[i] * pc_ref[b, 0:1, col(h)] + upd
        for i, (b, h) in enumerate(chains):
            mu = jnp.mean(y[i], axis=1, keepdims=True)
            var = jnp.mean(jnp.square(y[i] - mu), axis=1, keepdims=True)
            yn = (y[i] - mu) * lax.rsqrt(var + EPS) * lng_ref[:, col(h)] + lnb_ref[:, col(h)]
            y_ref[b, :, col(h)] = (g_ref[b, :, col(h)] * (yn + bonus_ref[b, :, col(h)])).astype(BF16)


def rwkv_scan(rt, kt, bt, ktt, v, ktp, bp, pc, g, bonus, ln_g, ln_b):
    b, t, wmix = rt.shape
    nh = wmix // HEAD_DIM
    c = RWKV_CHUNK
    row = pl.BlockSpec((b, c, wmix), lambda j: (0, j, 0))
    cvec = _const_spec((1, wmix))
    return pl.pallas_call(
        functools.partial(_rwkv_scan_kernel, nb=b, nh=nh, group=b * nh),
        grid=(t // c,),
        in_specs=[row] * 10 + [cvec, cvec],
        out_specs=row,
        out_shape=jax.ShapeDtypeStruct((b, t, wmix), BF16),
        scratch_shapes=[pltpu.VMEM((b * nh, HEAD_DIM, HEAD_DIM), F32)],
        compiler_params=_cp(1),
    )(rt, kt, bt, ktt, v, ktp, bp, pc, g, bonus, ln_g, ln_b)


def _lru_kernel(gd_ref, xd_ref, cw_ref, cb_ref, wa_ref, ba_ref, wx_ref, bx_ref, lam_ref, y_ref,
                xcarry_ref, hcarry_ref, *, tl, kconv):
    @pl.when(pl.program_id(1) == 0)
    def _():
        xcarry_ref[...] = jnp.zeros(xcarry_ref.shape, F32)
        hcarry_ref[...] = jnp.zeros(hcarry_ref.shape, F32)

    x = xd_ref[0]
    wdt = x.shape[1]
    row = _iota((tl, wdt), 0)
    prev = xcarry_ref[...]
    xc = cw_ref[kconv - 1:kconv, :] * x + cb_ref[...]
    for dly in range(1, kconv):
        sh = pltpu.roll(x, dly, 0)
        for r0 in range(dly):
            sh = jnp.where(row == r0, prev[8 - dly + r0:8 - dly + r0 + 1, :], sh)
        xc = xc + cw_ref[kconv - 1 - dly:kconv - dly, :] * sh
    xcarry_ref[...] = x[tl - 8:tl, :]

    xb = xc.astype(BF16)
    r = _sigmoid(_dot(xb, wa_ref[...]) + ba_ref[...])
    i = _sigmoid(_dot(xb, wx_ref[...]) + bx_ref[...])
    nl = -lam_ref[...]
    softplus = jnp.maximum(nl, 0.0) + jnp.log1p(jnp.exp(-jnp.abs(nl)))
    log_a = -LRU_C * r * softplus
    a = jnp.exp(log_a)
    th = jnp.tanh(log_a)
    bterm = jnp.sqrt(-2.0 * th / (1.0 - th)) * (i * xc)

    d = 1
    while d < tl:
        keep = row >= d
        a_sh = jnp.where(keep, pltpu.roll(a, d, 0), 1.0)
        b_sh = jnp.where(keep, pltpu.roll(bterm, d, 0), 0.0)
        bterm = a * b_sh + bterm
        a = a * a_sh
        d *= 2
    hseq = bterm + a * hcarry_ref[0:1, :]
    hcarry_ref[0:1, :] = hseq[tl - 1:tl, :]
    y_ref[0] = (_gelu(gd_ref[0]) * hseq).astype(BF16)


def rglru(gd, xd, conv_w, conv_b, wa_bd, ba, wx_bd, bx, lam, *, tl):
    b, t, wdt = xd.shape
    kconv = conv_w.shape[0]
    row = pl.BlockSpec((1, tl, wdt), lambda i, j: (i, j, 0))
    cvec = _const_spec((1, wdt))
    return pl.pallas_call(
        functools.partial(_lru_kernel, tl=tl, kconv=kconv),
        grid=(b, t // tl),
        in_specs=[row, row, _const_spec(conv_w.shape), cvec, _const_spec(wa_bd.shape), cvec,
                  _const_spec(wx_bd.shape), cvec, cvec],
        out_specs=row,
        out_shape=jax.ShapeDtypeStruct((b, t, wdt), BF16),
        scratch_shapes=[pltpu.VMEM((8, wdt), F32), pltpu.VMEM((8, wdt), F32)],
        compiler_params=_cp(2),
    )(gd, xd, conv_w, conv_b, wa_bd, ba, wx_bd, bx, lam)


def _block_diag(w):
    n, a, b = w.shape
    eye = jnp.eye(n, dtype=w.dtype)
    return (eye[:, None, :, None] * w[:, :, None, :]).reshape(n * a, n * b)


def _even_weights(w_in, mix_half):
    d = w_in.shape[0]
    nh = mix_half // HEAD_DIM
    hk = nh // NSA_GROUP
    nkv = hk * HEAD_DIM
    o = 0
    wq = w_in[:, o:o + mix_half].reshape(d, nh, HEAD_DIM); o += mix_half
    kvs = []
    for _ in range(6):
        kvs.append(w_in[:, o:o + nkv]); o += nkv
    wg = w_in[:, o:o + 3 * nh].reshape(d, hk, 3 * NSA_GROUP); o += 3 * nh
    wuv = w_in[:, o:]
    kv_of = jnp.arange(nh) // NSA_GROUP
    place = jax.nn.one_hot(kv_of, LANES // HEAD_DIM, dtype=w_in.dtype)
    wq_pad = (wq[:, :, None, :] * place[None, :, :, None]).reshape(d, nh * LANES)
    wg_pad = jnp.pad(wg, ((0, 0), (0, 0), (0, LANES - 3 * NSA_GROUP))).reshape(d, hk * LANES)
    return jnp.concatenate([wq_pad] + kvs + [wg_pad, wuv], axis=1).astype(BF16)


def _compress_weights(pe, w1, w2, hk):
    hid = w1.shape[1]
    pe2 = jnp.tile(pe, (1, hk))
    w1r = w1.reshape(CMP_BLOCK, HEAD_DIM, hid)
    eye = jnp.eye(hk, dtype=w1.dtype)
    w1_bd = (eye[None, :, None, :, None] * w1r[:, None, :, None, :]).reshape(CMP_BLOCK, hk * HEAD_DIM, hk * hid)
    w2_bd = (eye[:, None, :, None] * w2[None, :, None, :]).reshape(hk * hid, hk * HEAD_DIM)
    return pe2, w1_bd.astype(BF16), w2_bd.astype(BF16)


def _ffn_weights(w_up, conv_w, conv_b, w_down):
    return w_up.astype(BF16), conv_w, conv_b.reshape(1, -1), w_down.astype(BF16)


def kernel(x, c, positions, ada_w, ada_b, norm1_g, norm2_g, ev_w_in, ev_w_out, nsa_pe_k, nsa_w1_k, nsa_w2_k, nsa_pe_v, nsa_w1_v, nsa_w2_v, gmlp_ln_g, gmlp_ln_b, gmlp_ws, gmlp_bs, od_w_in, od_w_out, rwkv_mu, rwkv_w0, rwkv_wB, rwkv_a0, rwkv_aB, rwkv_gB, rwkv_xi, rwkv_alpha, rwkv_rho, rwkv_ln_g, rwkv_ln_b, lru_conv_w, lru_conv_b, lru_wa, lru_ba, lru_wx, lru_bx, lru_lambda, ffn_up, ffn_conv_w, ffn_conv_b, ffn_down, final_g):
    b, t, d = x.shape
    depth = ada_w.shape[0]
    mix_half = d // 2
    hk = mix_half // HEAD_DIM // NSA_GROUP
    assert hk * HEAD_DIM == LANES and t % 512 == 0
    tm = 512

    mod = adaln_mod(c, ada_w, ada_b)
    mods = mod.reshape(depth, b, 6, 1, d)
    tabs = rope_tables(positions, tm)
    n_cmp = (t - CMP_BLOCK) // CMP_STRIDE + 1
    ngrp = t // CMP_STRIDE
    pos_end = jnp.pad(positions[:, CMP_BLOCK - 1::CMP_STRIDE][:, :n_cmp], ((0, 0), (0, ngrp - n_cmp)))
    end_tabs = rope_tables(pos_end, ngrp)

    for layer in range(depth):
        i = layer // 2
        sh1, sc1, g1, sh2, sc2, g2 = (mods[layer, :, j] for j in range(6))
        n1 = norm1_g[layer].reshape(1, d)
        n2 = norm2_g[layer].reshape(1, d)
        if layer % 2 == 0:
            w_in = _even_weights(ev_w_in[i], mix_half)
            q, kc, vc, ks, vst, kw, vwt, gates, uv = even_proj(x, n1, sh1, sc1, w_in, tabs, tm=tm, tk_sel=512,
                                                             tk_win=256)
            pek, w1k, w2k = _compress_weights(nsa_pe_k[i], nsa_w1_k[i], nsa_w2_k[i], hk)
            pev, w1v, w2v = _compress_weights(nsa_pe_v[i], nsa_w1_v[i], nsa_w2_v[i], hk)
            kcmp, vcmp = nsa_compress(kc, vc, pek, w1k, w2k, pev, w1v, w2v, end_tabs)
            oc, sel_t = nsa_cmp_topk(q, kcmp, vcmp, tq=256)
            ya = nsa_sel_win(q, ks, vst, kw, vwt, sel_t, oc, gates, tq=256)
            yb = gmlp_gating(uv, gmlp_ln_g[i], gmlp_ln_b[i], gmlp_ws[i], gmlp_bs[i], tg=tm)
            wo = ev_w_out[i].astype(BF16)
        else:
            w_in = od_w_in[i].astype(BF16)
            lora = rwkv_wB.shape[1]
            wb = jnp.pad(rwkv_wB[i], ((0, LANES - lora), (0, 0))).astype(BF16)
            ab = jnp.pad(rwkv_aB[i], ((LANES - rwkv_aB.shape[1], 0), (0, 0))).astype(BF16)
            v1 = lambda p: p.reshape(1, -1)
            (rt, kt, bt, ktt, v, ktp, bp, pc, gout, bonus, gd, xd) = odd_proj(
                x, n1, sh1, sc1, w_in, v1(rwkv_mu[i]), v1(rwkv_w0[i]), wb, v1(rwkv_a0[i]), ab,
                rwkv_gB[i].astype(BF16), v1(rwkv_xi[i]), v1(rwkv_alpha[i]), v1(rwkv_rho[i]), tm=tm)
            ya = rwkv_scan(rt, kt, bt, ktt, v, ktp, bp, pc, gout, bonus, v1(rwkv_ln_g[i]), v1(rwkv_ln_b[i]))
            yb = rglru(gd, xd, lru_conv_w[i], v1(lru_conv_b[i]), _block_diag(lru_wa[i]).astype(BF16),
                       v1(lru_ba[i]), _block_diag(lru_wx[i]).astype(BF16), v1(lru_bx[i]), v1(lru_lambda[i]), tl=tm)
            wo = od_w_out[i].astype(BF16)
        wup, cw, cb, wdn = _ffn_weights(ffn_up[layer], ffn_conv_w[layer], ffn_conv_b[layer], ffn_down[layer])
        x = outproj_ffn(ya, yb, x, wo, g1, n2, sh2, sc2, g2, wup, cw, cb, wdn, final_g.reshape(1, d),
                        tm=tm, cwid=256, last_layer=layer == depth - 1)
    return x
```
